```python
import jax
import jax.numpy as jnp
from jax import lax
import numpy as np

D_MODEL = 4096
BATCH = 2
SEQ = 4096
DEPTH = 2

GRID_W = 64
CTX_LEN = 256

RWKV_DIM = D_MODEL // 4
RWKV_HEAD = 64
RWKV_HEADS = RWKV_DIM // RWKV_HEAD
RWKV_LORA_W = max(32, int(round(RWKV_DIM ** 0.5 * 1.8 / 32)) * 32)
RWKV_LORA_A = max(32, int(round(RWKV_DIM ** 0.5 * 1.8 / 32)) * 32)
RWKV_LORA_G = max(32, int(round(RWKV_DIM ** 0.6 * 0.8 / 32)) * 32)
RWKV_PROJ = 3 * RWKV_DIM + RWKV_LORA_W + RWKV_LORA_A + RWKV_LORA_G
RWKV_LN_EPS = 64e-5

RET_DIM = D_MODEL // 4
RET_V_HEAD = 256
RET_HEADS = RET_DIM // RET_V_HEAD
RET_QK_HEAD = RET_V_HEAD // 2
RET_CHUNK = 128
RET_PROJ = 2 * RET_HEADS * RET_QK_HEAD + 2 * RET_DIM
RET_NORM_EPS = 1e-6

ATT_DIM = D_MODEL // 2
ATT_HEAD = 128
ATT_HEADS = ATT_DIM // ATT_HEAD
ATT_KV_HEADS = ATT_HEADS // 4
ATT_GROUP = ATT_HEADS // ATT_KV_HEADS
ATT_BLOCK = 128
ATT_PROJ = ATT_DIM + 2 * ATT_KV_HEADS * ATT_HEAD

N_BRANCH = 3
MIX_PROJ = RWKV_PROJ + RET_PROJ + ATT_PROJ
N_IN = MIX_PROJ + N_BRANCH * D_MODEL

FFN_DIM = 256 * ((8 * D_MODEL // 3 + 255) // 256)
CONV_W = 3
ROPE_THETA = 10000.0
NORM_EPS = 1e-6

kernel_name = 'hybrid_rwkv7_retention_gqa_prefix_dit_block'


def rms_norm(x, w, eps=NORM_EPS):
    xf = x.astype(jnp.float32)
    y = xf * lax.rsqrt(jnp.mean(xf * xf, axis=-1, keepdims=True) + eps)
    return (y * w.astype(jnp.float32)).astype(x.dtype)


def head_layer_norm(y, eps):
    yf = y.astype(jnp.float32)
    yc = yf - jnp.mean(yf, axis=-1, keepdims=True)
    return yc * lax.rsqrt(jnp.mean(yc * yc, axis=-1, keepdims=True) + eps)


def shift_prev(h):
    return jnp.pad(h, ((0, 0), (1, 0), (0, 0)))[:, :-1]


def shift_next(h):
    return jnp.pad(h, ((0, 0), (0, 1), (0, 0)))[:, 1:]


def rope_angles(pos, dim):
    inv_freq = ROPE_THETA ** (-jnp.arange(0, dim, 2, dtype=jnp.float32) / dim)
    return pos.astype(jnp.float32)[:, None] * inv_freq[None, :]


def apply_rope(x, ang):
    cos = jnp.cos(ang)[None, :, None, :]
    sin = jnp.sin(ang)[None, :, None, :]
    x1, x2 = jnp.split(x.astype(jnp.float32), 2, axis=-1)
    return jnp.concatenate([x1 * cos - x2 * sin, x2 * cos + x1 * sin], axis=-1).astype(x.dtype)


def apply_axial_rope(x, ang_row, ang_col):
    x_row, x_col = jnp.split(x, 2, axis=-1)
    return jnp.concatenate([apply_rope(x_row, ang_row), apply_rope(x_col, ang_col)], axis=-1)


def rwkv7_scan(r, w, k, v, a, b, s0, reverse):
    def step(S, inp):
        r_t, w_t, k_t, v_t, a_t, b_t = inp
        sa = jnp.einsum('bhvk,bhk->bhv', S, a_t)
        S = S * w_t[:, :, None, :] + sa[..., None] * b_t[:, :, None, :] + v_t[..., None] * k_t[:, :, None, :]
        return S, jnp.einsum('bhvk,bhk->bhv', S, r_t)
    xs = tuple(jnp.swapaxes(t.astype(jnp.float32), 0, 1) for t in (r, w, k, v, a, b))
    s_final, ys = lax.scan(step, s0, xs, reverse=reverse)
    return jnp.swapaxes(ys, 0, 1), s_final


def rwkv_heads(t):
    return t.reshape(t.shape[0], t.shape[1], RWKV_HEADS, RWKV_HEAD)


def rwkv7_branch(p_x, p_c, mu_prev, mu_next, w0, w2, a0, a2, g2, k_k, k_a, r_k, ln_w, ln_b, ctx_out):
    cuts = [RWKV_DIM, 2 * RWKV_DIM, 3 * RWKV_DIM, 3 * RWKV_DIM + RWKV_LORA_W,
            3 * RWKV_DIM + RWKV_LORA_W + RWKV_LORA_A]

    def prepare(p):
        p = p + mu_prev * (shift_prev(p) - p) + mu_next * (shift_next(p) - p)
        r, k, v, w_low, a_low, g_low = jnp.split(p, cuts, axis=-1)
        kk = rwkv_heads(k * k_k).astype(jnp.float32)
        kk = kk / jnp.maximum(jnp.sqrt(jnp.sum(kk * kk, axis=-1, keepdims=True)), 1e-12)
        return r, k, v, w_low, a_low, g_low, kk

    def direction(streams, d, s0):
        r, k, v, w_low, a_low, _, kk = streams
        w_log = -jax.nn.softplus(-(w0[d] + jnp.tanh(w_low) @ w2[d]).astype(jnp.float32)) - 0.5
        decay = jnp.exp(-jnp.exp(w_log))
        a = jax.nn.sigmoid((a0[d] + a_low @ a2[d]).astype(jnp.float32))
        kd = rwkv_heads(k * (1.0 + (a - 1.0) * k_a))
        rh, vh = rwkv_heads(r), rwkv_heads(v)
        y, s = rwkv7_scan(rh, rwkv_heads(decay), kd, vh, -kk, kk * rwkv_heads(a), s0, reverse=(d == 1))
        bonus = jnp.sum(rh * kd * r_k, axis=-1, keepdims=True) * vh
        return y, bonus, s

    def finish(y, bonus, g_low):
        B, L = y.shape[:2]
        y = head_layer_norm(y, RWKV_LN_EPS).reshape(B, L, RWKV_DIM) * ln_w + ln_b
        y = y + bonus.reshape(B, L, RWKV_DIM)
        return (y * (jax.nn.sigmoid(g_low) @ g2)).astype(g_low.dtype)

    sx, sc = prepare(p_x), prepare(p_c)
    zero = jnp.zeros((p_x.shape[0], RWKV_HEADS, RWKV_HEAD, RWKV_HEAD), jnp.float32)
    yc_f, bc_f, s_f = direction(sc, 0, zero)
    yc_b, bc_b, s_b = direction(sc, 1, zero)
    yx_f, bx_f, _ = direction(sx, 0, s_f)
    yx_b, bx_b, _ = direction(sx, 1, s_b)
    out_x = finish(yx_f + yx_b, bx_f + bx_b, sx[5])
    out_c = finish(yc_f + yc_b, bc_f + bc_b, sc[5]) if ctx_out else None
    return out_x, out_c


def retention_chunks(q, k, v, log_gamma, s0):
    B, L, H, _ = q.shape
    dv = v.shape[-1]
    n = L // RET_CHUNK
    lg = log_gamma.astype(jnp.float32)
    idx = jnp.arange(RET_CHUNK, dtype=jnp.float32)
    diff = idx[:, None] - idx[None, :]
    inner_decay = jnp.where(diff[None] >= 0, jnp.exp(jnp.maximum(diff, 0.0)[None] * lg[:, None, None]), 0.0)
    q_decay = jnp.exp((idx[:, None] + 1.0) * lg[None, :])
    k_decay = jnp.exp((RET_CHUNK - 1.0 - idx)[:, None] * lg[None, :])
    chunk_decay = jnp.exp(RET_CHUNK * lg)

    def to_chunks(t):
        return jnp.swapaxes(t.astype(jnp.float32).reshape(B, n, RET_CHUNK, H, t.shape[-1]), 0, 1)

    def step(S, inp):
        qc, kc, vc = inp
        scores = jnp.einsum('bihd,bjhd->bhij', qc, kc) * inner_decay[None]
        o = (jnp.einsum('bhij,bjhe->bihe', scores, vc)
             + jnp.einsum('bihd,bhde->bihe', qc * q_decay[None, :, :, None], S))
        S = S * chunk_decay[None, :, None, None] + jnp.einsum('bjhd,bjhe->bhde', kc * k_decay[None, :, :, None], vc)
        return S, o

    s_final, o = lax.scan(step, s0, (to_chunks(q), to_chunks(k), to_chunks(v)))
    return jnp.swapaxes(o, 0, 1).reshape(B, L, H, dv), s_final


def retention_branch(p_x, p_c, ang_seq, log_decay, ctx_out):
    nqk = RET_HEADS * RET_QK_HEAD

    def prepare(p, ang):
        q, k, v, g = jnp.split(p, [nqk, 2 * nqk, 2 * nqk + RET_DIM], axis=-1)
        B, L = p.shape[:2]
        q = q.reshape(B, L, RET_HEADS, RET_QK_HEAD)
        k = k.reshape(B, L, RET_HEADS, RET_QK_HEAD) * (RET_QK_HEAD ** -0.5)
        v = v.reshape(B, L, RET_HEADS, RET_V_HEAD)
        if ang is not None:
            q = apply_rope(q, ang)
            k = apply_rope(k, ang)
        return q, k, v, g

    def flip(t):
        return jnp.flip(t, axis=1)

    def finish(o, g):
        B, L = o.shape[:2]
        y = head_layer_norm(o, RET_NORM_EPS).reshape(B, L, RET_DIM)
        return (y * jax.nn.silu(g)).astype(g.dtype)

    qx, kx, vx, gx = prepare(p_x, ang_seq)
    qc, kc, vc, gc = prepare(p_c, None)
    zero = jnp.zeros((p_x.shape[0], RET_HEADS, RET_QK_HEAD, RET_V_HEAD), jnp.float32)
    oc_f, s_f = retention_chunks(qc, kc, vc, log_decay[0], zero)
    ox_f, _ = retention_chunks(qx, kx, vx, log_decay[0], s_f)
    oc_b, s_b = retention_chunks(flip(qc), flip(kc), flip(vc), log_decay[1], zero)
    ox_b, _ = retention_chunks(flip(qx), flip(kx), flip(vx), log_decay[1], s_b)
    out_x = finish(ox_f + flip(ox_b), gx)
    out_c = finish(oc_f + flip(oc_b), gc) if ctx_out else None
    return out_x, out_c


def gqa_attend(q, k, v):
    s = jnp.einsum('bthgd,bkhd->bhgtk', q, k).astype(jnp.float32) * (ATT_HEAD ** -0.5)
    p = jax.nn.softmax(s, axis=-1).astype(v.dtype)
    return jnp.einsum('bhgtk,bkhd->bthgd', p, v)


def attention_branch(p_x, p_c, q_norm, k_norm, ang_row, ang_col, ctx_out):
    def prepare(p, with_rope):
        q, k, v = jnp.split(p, [ATT_DIM, ATT_DIM + ATT_KV_HEADS * ATT_HEAD], axis=-1)
        B, L = p.shape[:2]
        q = rms_norm(q.reshape(B, L, ATT_HEADS, ATT_HEAD), q_norm)
        k = rms_norm(k.reshape(B, L, ATT_KV_HEADS, ATT_HEAD), k_norm)
        v = v.reshape(B, L, ATT_KV_HEADS, ATT_HEAD)
        if with_rope:
            q = apply_axial_rope(q, ang_row, ang_col)
            k = apply_axial_rope(k, ang_row, ang_col)
        return q.reshape(B, L, ATT_KV_HEADS, ATT_GROUP, ATT_HEAD), k, v

    qx, kx, vx = prepare(p_x, True)
    qc, kc, vc = prepare(p_c, False)
    k_all = jnp.concatenate([kc, kx], axis=1)
    v_all = jnp.concatenate([vc, vx], axis=1)
    B, S = qx.shape[:2]
    nb = S // ATT_BLOCK
    q_blocks = jnp.swapaxes(qx.reshape(B, nb, ATT_BLOCK, ATT_KV_HEADS, ATT_GROUP, ATT_HEAD), 0, 1)
    o_blocks = lax.map(lambda qb: gqa_attend(qb, k_all, v_all), q_blocks)
    out_x = jnp.swapaxes(o_blocks, 0, 1).reshape(B, S, ATT_DIM)
    out_c = gqa_attend(qc, kc, vc).reshape(B, qc.shape[1], ATT_DIM) if ctx_out else None
    return out_x, out_c


def token_mixer(h_x, h_c, ang_row, ang_col, ang_seq, w_in, mu_prev, mu_next, w0, w2, a0, a2, g2, k_k, k_a,
                r_k, ln_w, ln_b, ret_log_decay, q_norm, k_norm, w_rwkv_out, w_ret_out, w_attn_out, w_out, ctx_out):
    cuts = [RWKV_PROJ, RWKV_PROJ + RET_PROJ, MIX_PROJ]
    px = jnp.split(h_x @ w_in, cuts, axis=-1)
    w_in_c = w_in if ctx_out else w_in[:, :MIX_PROJ]
    pc = jnp.split(h_c @ w_in_c, cuts, axis=-1)
    rw_x, rw_c = rwkv7_branch(px[0], pc[0], mu_prev, mu_next, w0, w2, a0, a2, g2, k_k, k_a, r_k, ln_w, ln_b, ctx_out)
    rt_x, rt_c = retention_branch(px[1], pc[1], ang_seq, ret_log_decay, ctx_out)
    at_x, at_c = attention_branch(px[2], pc[2], q_norm, k_norm, ang_row, ang_col, ctx_out)

    def merge(rw, rt, at, gate_logits):
        B, L = gate_logits.shape[:2]
        gates = jax.nn.sigmoid(gate_logits.reshape(B, L, N_BRANCH, D_MODEL))
        m = (gates[:, :, 0] * (rw @ w_rwkv_out) + gates[:, :, 1] * (rt @ w_ret_out)
             + gates[:, :, 2] * (at @ w_attn_out))
        return m @ w_out

    out_x = merge(rw_x, rt_x, at_x, px[3])
    out_c = merge(rw_c, rt_c, at_c, pc[3]) if ctx_out else None
    return out_x, out_c


def conv_ffn(h, w_up, conv_w, conv_b, w_down):
    u = h @ w_up
    u = shift_prev(u) * conv_w[0] + u * conv_w[1] + shift_next(u) * conv_w[2] + conv_b
    val, gate = jnp.split(u, 2, axis=-1)
    return (jax.nn.silu(gate) * val) @ w_down


def setup_inputs(seed: int = 0) -> dict:
    key = jax.random.key(seed)
    ks = iter(jax.random.split(key, 48))
    f32 = jnp.float32

    def nrm(shape, scale):
        return jax.random.normal(next(ks), shape, f32) * scale

    def unif(shape, lo, hi):
        return jax.random.uniform(next(ks), shape, f32, lo, hi)

    ret_base = jnp.log(1.0 - 2.0 ** (-5.0 - jnp.arange(RET_HEADS, dtype=f32)))
    return {
        'x': nrm((BATCH, SEQ, D_MODEL), 1.0),
        'c': nrm((BATCH, D_MODEL), 1.0),
        'ctx': nrm((BATCH, CTX_LEN, D_MODEL), 1.0),
        'c_ctx': nrm((D_MODEL,), 1.0),
        'w_ada': nrm((DEPTH, D_MODEL, 6 * D_MODEL), 0.5 * D_MODEL ** -0.5),
        'b_ada': nrm((DEPTH, 6 * D_MODEL), 0.02),
        'norm1_w': 1.0 + nrm((DEPTH, D_MODEL), 0.02),
        'norm2_w': 1.0 + nrm((DEPTH, D_MODEL), 0.02),
        'w_in': nrm((DEPTH, D_MODEL, N_IN), D_MODEL ** -0.5),
        'rwkv_mu_prev': unif((DEPTH, RWKV_PROJ), 0.0, 0.5),
        'rwkv_mu_next': unif((DEPTH, RWKV_PROJ), 0.0, 0.5),
        'rwkv_w0': -6.5 + 5.0 * unif((DEPTH, 2, RWKV_DIM), 0.0, 1.0),
        'rwkv_w2': nrm((DEPTH, 2, RWKV_LORA_W, RWKV_DIM), 0.5 * RWKV_LORA_W ** -0.5),
        'rwkv_a0': nrm((DEPTH, 2, RWKV_DIM), 0.1),
        'rwkv_a2': nrm((DEPTH, 2, RWKV_LORA_A, RWKV_DIM), 0.5 * RWKV_LORA_A ** -0.5),
        'rwkv_g2': nrm((DEPTH, RWKV_LORA_G, RWKV_DIM), RWKV_LORA_G ** -0.5),
        'rwkv_k_k': 0.85 + nrm((DEPTH, RWKV_DIM), 0.02),
        'rwkv_k_a': 1.0 + nrm((DEPTH, RWKV_DIM), 0.02),
        'rwkv_r_k': nrm((DEPTH, RWKV_HEADS, RWKV_HEAD), 0.1),
        'rwkv_ln_w': 1.0 + nrm((DEPTH, RWKV_DIM), 0.02),
        'rwkv_ln_b': nrm((DEPTH, RWKV_DIM), 0.02),
        'w_rwkv_out': nrm((DEPTH, RWKV_DIM, D_MODEL), RWKV_DIM ** -0.5),
        'ret_log_decay': ret_base * (1.0 + nrm((DEPTH, 2, RET_HEADS), 0.05)),
        'w_ret_out': nrm((DEPTH, RET_DIM, D_MODEL), RET_DIM ** -0.5),
        'attn_q_norm': 1.0 + nrm((DEPTH, ATT_HEAD), 0.02),
        'attn_k_norm': 1.0 + nrm((DEPTH, ATT_HEAD), 0.02),
        'w_attn_out': nrm((DEPTH, ATT_DIM, D_MODEL), ATT_DIM ** -0.5),
        'w_out': nrm((DEPTH, D_MODEL, D_MODEL), D_MODEL ** -0.5),
        'w_ffn_up': nrm((DEPTH, D_MODEL, 2 * FFN_DIM), D_MODEL ** -0.5),
        'ffn_conv_w': nrm((DEPTH, CONV_W, 2 * FFN_DIM), CONV_W ** -0.5),
        'ffn_conv_b': nrm((DEPTH, 2 * FFN_DIM), 0.02),
        'w_ffn_down': nrm((DEPTH, FFN_DIM, D_MODEL), FFN_DIM ** -0.5),
    }


def reference(x, c, ctx, c_ctx, w_ada, b_ada, norm1_w, norm2_w, w_in, rwkv_mu_prev, rwkv_mu_next, rwkv_w0,
              rwkv_w2, rwkv_a0, rwkv_a2, rwkv_g2, rwkv_k_k, rwkv_k_a, rwkv_r_k, rwkv_ln_w, rwkv_ln_b, w_rwkv_out,
              ret_log_decay, w_ret_out, attn_q_norm, attn_k_norm, w_attn_out, w_out, w_ffn_up, ffn_conv_w,
              ffn_conv_b, w_ffn_down):
    seq = x.shape[1]
    rows = seq // GRID_W
    row = jnp.repeat(jnp.arange(rows), GRID_W)
    col = jnp.tile(jnp.arange(GRID_W), rows)
    ang_row = rope_angles(row, ATT_HEAD // 2)
    ang_col = rope_angles(col, ATT_HEAD // 2)
    ang_seq = rope_angles(jnp.arange(seq), RET_QK_HEAD)
    silu_c = jax.nn.silu(c)
    silu_cc = jax.nn.silu(c_ctx)
    h_ctx = ctx
    for l in range(DEPTH):
        ctx_out = l < DEPTH - 1
        mod_x = (silu_c @ w_ada[l] + b_ada[l])[:, None, :]
        mod_c = (silu_cc @ w_ada[l] + b_ada[l])[None, None, :]
        sh1, sc1, g1, sh2, sc2, g2 = jnp.split(mod_x, 6, axis=-1)
        csh1, csc1, cg1, csh2, csc2, cg2 = jnp.split(mod_c, 6, axis=-1)
        n_x = rms_norm(x, norm1_w[l]) * (1.0 + sc1) + sh1
        n_c = rms_norm(h_ctx, norm1_w[l]) * (1.0 + csc1) + csh1
        m_x, m_c = token_mixer(n_x, n_c, ang_row, ang_col, ang_seq, w_in[l], rwkv_mu_prev[l], rwkv_mu_next[l],
                               rwkv_w0[l], rwkv_w2[l], rwkv_a0[l], rwkv_a2[l], rwkv_g2[l], rwkv_k_k[l],
                               rwkv_k_a[l], rwkv_r_k[l], rwkv_ln_w[l], rwkv_ln_b[l], ret_log_decay[l],
                               attn_q_norm[l], attn_k_norm[l], w_rwkv_out[l], w_ret_out[l], w_attn_out[l],
                               w_out[l], ctx_out)
        x = x + g1 * m_x
        x = x + g2 * conv_ffn(rms_norm(x, norm2_w[l]) * (1.0 + sc2) + sh2,
                              w_ffn_up[l], ffn_conv_w[l], ffn_conv_b[l], w_ffn_down[l])
        if ctx_out:
            h_ctx = h_ctx + cg1 * m_c
            h_ctx = h_ctx + cg2 * conv_ffn(rms_norm(h_ctx, norm2_w[l]) * (1.0 + csc2) + csh2,
                                           w_ffn_up[l], ffn_conv_w[l], ffn_conv_b[l], w_ffn_down[l])
    return x
```

```python
import functools

import jax
import jax.numpy as jnp
from jax import lax
from jax.experimental import pallas as pl
from jax.experimental.pallas import tpu as pltpu

F32 = jnp.float32
BF16 = jnp.bfloat16

V7X_VMEM_BYTES = 64 * 1024 * 1024
V7X_VMEM_LIMIT_CAP = 56 * 1024 * 1024
LANE = 128

GRID_W = 64
RWKV_HEAD = 64
RWKV_LN_EPS = 64e-5
RWKV_CHUNK = 64
RET_V_HEAD = 256
RET_CHUNK = 128
RET_NORM_EPS = 1e-6
ATT_HEAD = 128
ATT_GROUP = 4
ROPE_THETA = 10000.0
NORM_EPS = 1e-6
CONV_W = 3


def _vmem_limit(tile_bytes):
    return int(min(max(2 * tile_bytes + (8 << 20), 16 << 20), V7X_VMEM_LIMIT_CAP))


def _mm_kernel(a_ref, b_ref, o_ref, *, nk):
    part = jnp.dot(a_ref[...].astype(BF16), b_ref[...].astype(BF16), preferred_element_type=F32)
    if nk == 1:
        o_ref[...] = part.astype(o_ref.dtype)
    else:
        k = pl.program_id(2)

        @pl.when(k == 0)
        def _():
            o_ref[...] = part

        @pl.when(k > 0)
        def _():
            o_ref[...] += part


def _pick(n, prefs):
    for p in prefs:
        if n % p == 0:
            return p
    return n


def matmul(a, b, *, tm=None, tn=None, tk=None, out_dtype=F32):
    M, K = a.shape
    K2, N = b.shape
    assert K == K2
    tm = tm or _pick(M, (1088, 1024, 512, 256, 128, 8))
    tn = tn or _pick(N, (1024, 640, 512, 256, 128))
    tk = tk or K
    assert M % tm == 0 and N % tn == 0 and K % tk == 0, (M, N, K, tm, tn, tk)
    nk = K // tk
    if nk > 1:
        assert out_dtype == F32
    tile_bytes = (tm * tk * a.dtype.itemsize + tk * tn * b.dtype.itemsize
                  + tm * tn * (jnp.dtype(out_dtype).itemsize + 4))
    return pl.pallas_call(
        functools.partial(_mm_kernel, nk=nk),
        grid=(M // tm, N // tn, nk),
        in_specs=[pl.BlockSpec((tm, tk), lambda i, j, k: (i, k)),
                  pl.BlockSpec((tk, tn), lambda i, j, k: (k, j))],
        out_specs=pl.BlockSpec((tm, tn), lambda i, j, k: (i, j)),
        out_shape=jax.ShapeDtypeStruct((M, N), out_dtype),
        compiler_params=pltpu.CompilerParams(
            dimension_semantics=("parallel", "parallel", "arbitrary"),
            vmem_limit_bytes=_vmem_limit(tile_bytes)),
        name="matmul",
    )(a, b)


def _merge_kernel(rw_ref, rt_ref, at_ref, w1_ref, w2_ref, w3_ref, g1_ref, g2_ref, g3_ref, o_ref):
    def br(x_ref, w_ref, g_ref):
        y = jnp.dot(x_ref[...], w_ref[...], preferred_element_type=F32)
        return jax.nn.sigmoid(g_ref[...]) * y

    m = br(rw_ref, w1_ref, g1_ref) + br(rt_ref, w2_ref, g2_ref) + br(at_ref, w3_ref, g3_ref)
    o_ref[...] = m.astype(o_ref.dtype)


def merge_branches(rw, rt, at, w1, w2, w3, p_all, gate_col0, *, tm, tn=512):
    M = rw.shape[0]
    D = w1.shape[1]
    assert M % tm == 0 and D % tn == 0 and gate_col0 % tn == 0
    g0 = gate_col0 // tn
    nd = D // tn
    k1, k2, k3 = rw.shape[1], rt.shape[1], at.shape[1]
    tile_bytes = (tm * (k1 + k2 + k3) * 2 + (k1 + k2 + k3) * tn * 2 + 3 * tm * tn * 4 + tm * tn * 2
                  + 3 * tm * tn * 4)

    def gspec(i):
        return pl.BlockSpec((tm, tn), lambda m, n, i=i: (m, g0 + i * nd + n))

    return pl.pallas_call(
        _merge_kernel,
        grid=(M // tm, nd),
        in_specs=[pl.BlockSpec((tm, k1), lambda m, n: (m, 0)),
                  pl.BlockSpec((tm, k2), lambda m, n: (m, 0)),
                  pl.BlockSpec((tm, k3), lambda m, n: (m, 0)),
                  pl.BlockSpec((k1, tn), lambda m, n: (0, n)),
                  pl.BlockSpec((k2, tn), lambda m, n: (0, n)),
                  pl.BlockSpec((k3, tn), lambda m, n: (0, n)),
                  gspec(0), gspec(1), gspec(2)],
        out_specs=pl.BlockSpec((tm, tn), lambda m, n: (m, n)),
        out_shape=jax.ShapeDtypeStruct((M, D), BF16),
        compiler_params=pltpu.CompilerParams(
            dimension_semantics=("parallel", "parallel"),
            vmem_limit_bytes=_vmem_limit(tile_bytes)),
        name="merge_branches",
    )(rw, rt, at, w1, w2, w3, p_all, p_all, p_all)


def _dot_hi(x, y, dims=(((1,), (0,)), ((), ()))):
    return lax.dot_general(x, y, dims, precision=lax.Precision.HIGHEST, preferred_element_type=F32)


_NT = (((1,), (1,)), ((), ()))
_TN = (((0,), (0,)), ((), ()))


def _rwkv_kernel(r_ref, lw_ref, k_ref, v_ref, a_ref, b_ref, y_ref, st_ref, *, hb, C):
    ci = pl.program_id(1)

    @pl.when(ci == 0)
    def _():
        st_ref[...] = jnp.zeros_like(st_ref)

    row = lax.broadcasted_iota(jnp.int32, (C, C), 0)
    col = lax.broadcasted_iota(jnp.int32, (C, C), 1)
    lower = row >= col
    strict = row > col
    tri = lower.astype(F32)
    eye = (row == col).astype(F32)

    for h in range(hb):
        r = r_ref[h]
        lw = lw_ref[h]
        k = k_ref[h]
        v = v_ref[h]
        a = a_ref[h]
        b = b_ref[h]
        st = st_ref[h]

        g_in = _dot_hi(tri, lw)
        g_ex = g_in - lw
        g_last = g_in[C - 1:C, :]
        e_neg = jnp.exp(-g_in)
        a_t = a * jnp.exp(g_ex)
        r_t = r * jnp.exp(g_in)
        b_t = b * e_neg
        k_t = k * e_neg
        e_rem = jnp.exp(g_last - g_in)
        b_h = b * e_rem
        k_h = k * e_rem

        a_ab = jnp.where(strict, _dot_hi(a_t, b_t, _NT), 0.0)
        a_ak = jnp.where(strict, _dot_hi(a_t, k_t, _NT), 0.0)
        a_rb = jnp.where(lower, _dot_hi(r_t, b_t, _NT), 0.0)
        a_rk = jnp.where(lower, _dot_hi(r_t, k_t, _NT), 0.0)

        inv = eye + a_ab
        pw = a_ab
        n = 2
        while n < C:
            pw = _dot_hi(pw, pw)
            inv = inv + _dot_hi(inv, pw)
            n *= 2

        u = _dot_hi(inv, _dot_hi(a_t, st) + _dot_hi(a_ak, v))
        y = _dot_hi(r_t, st) + _dot_hi(a_rb, u) + _dot_hi(a_rk, v)
        y_ref[h] = y

        e_col = jnp.sum(eye * jnp.exp(g_last), axis=1, keepdims=True)
        st_ref[h] = e_col * st + _dot_hi(b_h, u, _TN) + _dot_hi(k_h, v, _TN)


def rwkv_scan(r, lw, k, v, a, b, *, hb=4, C=RWKV_CHUNK):
    G, T, N = r.shape
    assert G % hb == 0 and T % C == 0
    spec = pl.BlockSpec((hb, C, N), lambda g, c: (g, c, 0))
    return pl.pallas_call(
        functools.partial(_rwkv_kernel, hb=hb, C=C),
        grid=(G // hb, T // C),
        in_specs=[spec] * 6,
        out_specs=spec,
        out_shape=jax.ShapeDtypeStruct((G, T, N), F32),
        scratch_shapes=[pltpu.VMEM((hb, N, N), F32)],
        compiler_params=pltpu.CompilerParams(dimension_semantics=("parallel", "arbitrary")),
        name="rwkv_scan",
    )(r, lw, k, v, a, b)


def _ret_kernel(lg_ref, q_ref, k_ref, v_ref, o_ref, s_ref, *, C):
    ci = pl.program_id(1)

    @pl.when(ci == 0)
    def _():
        s_ref[...] = jnp.zeros_like(s_ref)

    lg = lg_ref[0][:, 0:1]
    q = q_ref[0]
    k = k_ref[0]
    v = v_ref[0]
    s = s_ref[...]
    row = lax.broadcasted_iota(jnp.int32, (C, C), 0)
    col = lax.broadcasted_iota(jnp.int32, (C, C), 1)
    diff = (row - col).astype(F32)
    inner = jnp.where(row >= col, jnp.exp(jnp.maximum(diff, 0.0) * lg), 0.0)
    idx = lax.broadcasted_iota(jnp.int32, (C, 1), 0).astype(F32)
    q_decay = jnp.exp((idx + 1.0) * lg)
    k_decay = jnp.exp((C - 1.0 - idx) * lg)
    chunk_decay = jnp.exp(C * lg)

    scores = _dot_hi(q, k, _NT) * inner
    o_ref[0] = _dot_hi(scores, v) + _dot_hi(q * q_decay, s)
    s_ref[...] = s * chunk_decay + _dot_hi(k * k_decay, v, _TN)


def retention_scan(q, k, v, lg, *, C=RET_CHUNK):
    G, T, dk = q.shape
    dv = v.shape[-1]
    assert T % C == 0
    lg_b = jnp.broadcast_to(lg.astype(F32)[:, None, None], (G, 1, LANE))
    return pl.pallas_call(
        functools.partial(_ret_kernel, C=C),
        grid=(G, T // C),
        in_specs=[pl.BlockSpec((1, 1, LANE), lambda g, c: (g, 0, 0)),
                  pl.BlockSpec((1, C, dk), lambda g, c: (g, c, 0)),
                  pl.BlockSpec((1, C, dk), lambda g, c: (g, c, 0)),
                  pl.BlockSpec((1, C, dv), lambda g, c: (g, c, 0))],
        out_specs=pl.BlockSpec((1, C, dv), lambda g, c: (g, c, 0)),
        out_shape=jax.ShapeDtypeStruct((G, T, dv), F32),
        scratch_shapes=[pltpu.VMEM((dk, dv), F32)],
        compiler_params=pltpu.CompilerParams(dimension_semantics=("parallel", "arbitrary")),
        name="retention_scan",
    )(lg_b, q, k, v)


def _attn_kernel(q_ref, k_ref, v_ref, o_ref, *, group, dh, scale):
    k = k_ref[0]
    v = v_ref[0]
    for g in range(group):
        q = q_ref[0, :, g * dh:(g + 1) * dh]
        s = lax.dot_general(q, k, _NT, preferred_element_type=F32) * scale
        m = jnp.max(s, axis=-1, keepdims=True)
        p = jnp.exp(s - m)
        l = jnp.sum(p, axis=-1, keepdims=True)
        o = jnp.dot(p.astype(BF16), v, preferred_element_type=F32)
        o_ref[0, :, g * dh:(g + 1) * dh] = (o / l).astype(o_ref.dtype)


def gqa_attention(q, k, v, *, tq=256, group=ATT_GROUP, dh=ATT_HEAD):
    B, Tq, HD = q.shape
    Tk = k.shape[1]
    hkv = k.shape[2] // dh
    assert HD == hkv * group * dh and Tq % tq == 0
    gw = group * dh
    tile_bytes = tq * gw * 2 * 2 + 2 * Tk * dh * 2 + 3 * tq * Tk * 4
    return pl.pallas_call(
        functools.partial(_attn_kernel, group=group, dh=dh, scale=dh ** -0.5),
        grid=(B, hkv, Tq // tq),
        in_specs=[pl.BlockSpec((1, tq, gw), lambda b, h, i: (b, i, h)),
                  pl.BlockSpec((1, Tk, dh), lambda b, h, i: (b, 0, h)),
                  pl.BlockSpec((1, Tk, dh), lambda b, h, i: (b, 0, h))],
        out_specs=pl.BlockSpec((1, tq, gw), lambda b, h, i: (b, i, h)),
        out_shape=jax.ShapeDtypeStruct((B, Tq, HD), BF16),
        compiler_params=pltpu.CompilerParams(
            dimension_semantics=("parallel", "parallel", "parallel"),
            vmem_limit_bytes=_vmem_limit(tile_bytes)),
        name="gqa_attention",
    )(q, k, v)


def _rms_norm(x, w, eps=NORM_EPS):
    return x * lax.rsqrt(jnp.mean(x * x, axis=-1, keepdims=True) + eps) * w


def _head_ln(y, eps):
    yc = y - jnp.mean(y, axis=-1, keepdims=True)
    return yc * lax.rsqrt(jnp.mean(yc * yc, axis=-1, keepdims=True) + eps)


def _rope_angles(pos, dim):
    inv_freq = ROPE_THETA ** (-jnp.arange(0, dim, 2, dtype=F32) / dim)
    return pos.astype(F32)[:, None] * inv_freq[None, :]


def _rope(x, cos, sin):
    x1, x2 = jnp.split(x, 2, axis=-1)
    c = cos[None, :, None, :]
    s = sin[None, :, None, :]
    return jnp.concatenate([x1 * c - x2 * s, x2 * c + x1 * s], axis=-1)


def _seg_shift_prev(h, starts):
    sp = jnp.pad(h, ((0, 0), (1, 0), (0, 0)))[:, :-1]
    return sp * (1.0 - starts)[None, :, None]


def _seg_shift_next(h, ends):
    sn = jnp.pad(h, ((0, 0), (0, 1), (0, 0)))[:, 1:]
    return sn * (1.0 - ends)[None, :, None]


def _seg_flip(t, n_ctx):
    return jnp.concatenate([jnp.flip(t[:, :n_ctx], axis=1), jnp.flip(t[:, n_ctx:], axis=1)], axis=1)


def kernel(x, c, ctx, c_ctx, w_ada, b_ada, norm1_w, norm2_w, w_in, rwkv_mu_prev, rwkv_mu_next, rwkv_w0, rwkv_w2, rwkv_a0, rwkv_a2, rwkv_g2, rwkv_k_k, rwkv_k_a, rwkv_r_k, rwkv_ln_w, rwkv_ln_b, w_rwkv_out, ret_log_decay, w_ret_out, attn_q_norm, attn_k_norm, w_attn_out, w_out, w_ffn_up, ffn_conv_w, ffn_conv_b, w_ffn_down):
    B, S, D = x.shape
    NC = ctx.shape[1]
    T = NC + S
    depth = w_in.shape[0]
    rw_dim = w_rwkv_out.shape[1]
    rw_heads = rw_dim // RWKV_HEAD
    lora_w = rwkv_w2.shape[2]
    lora_a = rwkv_a2.shape[2]
    rw_proj = rwkv_mu_prev.shape[1]
    ret_dim = w_ret_out.shape[1]
    ret_heads = ret_dim // RET_V_HEAD
    ret_qk = RET_V_HEAD // 2
    nqk = ret_heads * ret_qk
    ret_proj = 2 * nqk + 2 * ret_dim
    att_dim = w_attn_out.shape[1]
    att_heads = att_dim // ATT_HEAD
    att_kv = att_heads // ATT_GROUP
    att_proj = att_dim + 2 * att_kv * ATT_HEAD
    ffn_dim = w_ffn_down.shape[1]

    rw_pad = -rw_proj % LANE
    col_ret = rw_proj + rw_pad
    col_att = col_ret + ret_proj
    gate_pad = -(col_att + att_proj) % 512
    col_gate = col_att + att_proj + gate_pad

    rows = S // GRID_W
    pos = jnp.arange(S)
    ang_row = _rope_angles(pos // GRID_W, ATT_HEAD // 2)
    ang_col = _rope_angles(pos % GRID_W, ATT_HEAD // 2)
    ang_seq = _rope_angles(pos, ret_qk)

    def with_ctx(ang):
        return jnp.concatenate([jnp.zeros((NC, ang.shape[1]), F32), ang], axis=0)

    cos_row, sin_row = jnp.cos(with_ctx(ang_row)), jnp.sin(with_ctx(ang_row))
    cos_col, sin_col = jnp.cos(with_ctx(ang_col)), jnp.sin(with_ctx(ang_col))
    cos_seq, sin_seq = jnp.cos(with_ctx(ang_seq)), jnp.sin(with_ctx(ang_seq))

    tpos = jnp.arange(T)
    seg_start = ((tpos == 0) | (tpos == NC)).astype(F32)
    seg_end = ((tpos == NC - 1) | (tpos == T - 1)).astype(F32)

    silu_all = jnp.zeros((8, D), F32).at[:B].set(jax.nn.silu(c)).at[B].set(jax.nn.silu(c_ctx))

    def conv_ffn(h, l, starts, ends):
        Bh, L, _ = h.shape
        u = matmul(h.reshape(Bh * L, D).astype(BF16), w_ffn_up[l].astype(BF16), tn=512)
        u = u.reshape(Bh, L, 2 * ffn_dim)
        cw = ffn_conv_w[l]
        u = _seg_shift_prev(u, starts) * cw[0] + u * cw[1] + _seg_shift_next(u, ends) * cw[2] + ffn_conv_b[l]
        val, gate = jnp.split(u, 2, axis=-1)
        act = (jax.nn.silu(gate) * val).astype(BF16).reshape(Bh * L, ffn_dim)
        y = matmul(act, w_ffn_down[l].astype(BF16), tn=512, tk=ffn_dim // 2)
        return y.reshape(Bh, L, D)

    h_ctx = ctx
    for l in range(depth):
        ctx_out = l < depth - 1
        mod = matmul(silu_all, w_ada[l], tm=8, tn=1024) + b_ada[l][None, :]
        sh1, sc1, g1, sh2, sc2, g2 = jnp.split(mod[:B, None, :], 6, axis=-1)
        csh1, csc1, cg1, csh2, csc2, cg2 = jnp.split(mod[B][None, None, :], 6, axis=-1)

        n_x = _rms_norm(x, norm1_w[l]) * (1.0 + sc1) + sh1
        n_c = _rms_norm(h_ctx, norm1_w[l]) * (1.0 + csc1) + csh1
        n_all = jnp.concatenate([n_c, n_x], axis=1).astype(BF16).reshape(B * T, D)

        wl = w_in[l]
        mix_proj = rw_proj + ret_proj + att_proj
        w_in_p = jnp.concatenate(
            [wl[:, :rw_proj], jnp.zeros((D, rw_pad), F32), wl[:, rw_proj:mix_proj],
             jnp.zeros((D, gate_pad), F32), wl[:, mix_proj:]], axis=1).astype(BF16)
        p_all = matmul(n_all, w_in_p, tn=512)
        p3 = p_all.reshape(B, T, -1)

        p = p3[..., :rw_proj]
        p = (p + rwkv_mu_prev[l] * (_seg_shift_prev(p, seg_start) - p)
             + rwkv_mu_next[l] * (_seg_shift_next(p, seg_end) - p))
        cuts = [rw_dim, 2 * rw_dim, 3 * rw_dim, 3 * rw_dim + lora_w, 3 * rw_dim + lora_w + lora_a]
        r_, k_, v_, w_low, a_low, g_low = jnp.split(p, cuts, axis=-1)

        def heads(t):
            return t.reshape(B, T, rw_heads, RWKV_HEAD)

        kk = heads(k_ * rwkv_k_k[l])
        kk = kk / jnp.maximum(jnp.sqrt(jnp.sum(kk * kk, axis=-1, keepdims=True)), 1e-12)
        rh, vh = heads(r_), heads(v_)
        tanh_w = jnp.tanh(w_low).reshape(B * T, lora_w).astype(BF16)
        a_low2 = a_low.reshape(B * T, lora_a).astype(BF16)

        seqs = []
        bonus = 0.0
        for d in range(2):
            w_pre = rwkv_w0[l, d] + matmul(tanh_w, rwkv_w2[l, d].astype(BF16)).reshape(B, T, rw_dim)
            w_log = -jax.nn.softplus(-w_pre) - 0.5
            lw = -jnp.exp(w_log)
            a_sig = jax.nn.sigmoid(
                rwkv_a0[l, d] + matmul(a_low2, rwkv_a2[l, d].astype(BF16)).reshape(B, T, rw_dim))
            kd = heads(k_ * (1.0 + (a_sig - 1.0) * rwkv_k_a[l]))
            bonus = bonus + jnp.sum(rh * kd * rwkv_r_k[l], axis=-1, keepdims=True) * vh
            streams = (rh, heads(lw), kd, vh, -kk, kk * heads(a_sig))
            if d == 1:
                streams = tuple(_seg_flip(t, NC) for t in streams)
            seqs.append(streams)
        scan_in = [jnp.stack([seqs[0][i], seqs[1][i]], axis=0).transpose(0, 1, 3, 2, 4)
                   .reshape(2 * B * rw_heads, T, RWKV_HEAD) for i in range(6)]
        y_scan = rwkv_scan(*scan_in).reshape(2, B, rw_heads, T, RWKV_HEAD).transpose(0, 1, 3, 2, 4)
        y_rw = y_scan[0] + _seg_flip(y_scan[1], NC)
        y_rw = _head_ln(y_rw, RWKV_LN_EPS).reshape(B, T, rw_dim) * rwkv_ln_w[l] + rwkv_ln_b[l]
        y_rw = y_rw + bonus.reshape(B, T, rw_dim)
        g_gate = matmul(jax.nn.sigmoid(g_low).reshape(B * T, -1).astype(BF16),
                        rwkv_g2[l].astype(BF16)).reshape(B, T, rw_dim)
        rw_out = (y_rw * g_gate).astype(BF16)

        pr = p3[..., col_ret:col_ret + ret_proj]
        q_, k_r, v_r, g_r = jnp.split(pr, [nqk, 2 * nqk, 2 * nqk + ret_dim], axis=-1)
        q_ = _rope(q_.reshape(B, T, ret_heads, ret_qk), cos_seq, sin_seq)
        k_r = _rope(k_r.reshape(B, T, ret_heads, ret_qk) * (ret_qk ** -0.5), cos_seq, sin_seq)
        v_r = v_r.reshape(B, T, ret_heads, RET_V_HEAD)

        def ret_layout(t):
            t2 = jnp.stack([t, _seg_flip(t, NC)], axis=0).transpose(0, 1, 3, 2, 4)
            return t2.reshape(2 * B * ret_heads, T, t.shape[-1])

        lg = jnp.broadcast_to(ret_log_decay[l][:, None, :], (2, B, ret_heads)).reshape(-1)
        o_ret = retention_scan(ret_layout(q_), ret_layout(k_r), ret_layout(v_r), lg)
        o_ret = o_ret.reshape(2, B, ret_heads, T, RET_V_HEAD).transpose(0, 1, 3, 2, 4)
        o_ret = o_ret[0] + _seg_flip(o_ret[1], NC)
        rt_out = (_head_ln(o_ret, RET_NORM_EPS).reshape(B, T, ret_dim) * jax.nn.silu(g_r)).astype(BF16)

        pa = p3[..., col_att:col_att + att_proj]
        qa, ka, va = jnp.split(pa, [att_dim, att_dim + att_kv * ATT_HEAD], axis=-1)
        qa = _rms_norm(qa.reshape(B, T, att_heads, ATT_HEAD), attn_q_norm[l])
        ka = _rms_norm(ka.reshape(B, T, att_kv, ATT_HEAD), attn_k_norm[l])

        def axial(t):
            t_row, t_col = jnp.split(t, 2, axis=-1)
            return jnp.concatenate([_rope(t_row, cos_row, sin_row), _rope(t_col, cos_col, sin_col)], axis=-1)

        qa = axial(qa).reshape(B, T, att_dim).astype(BF16)
        ka = axial(ka).reshape(B, T, att_kv * ATT_HEAD).astype(BF16)
        va = va.astype(BF16)
        at_x = gqa_attention(qa[:, NC:], ka, va)
        if ctx_out:
            at_c = gqa_attention(qa[:, :NC], ka[:, :NC], va[:, :NC])
            at_out = jnp.concatenate([at_c, at_x], axis=1)
        else:
            at_out = at_x

        wo = [w_rwkv_out[l].astype(BF16), w_ret_out[l].astype(BF16), w_attn_out[l].astype(BF16)]
        if ctx_out:
            m = merge_branches(rw_out.reshape(B * T, rw_dim), rt_out.reshape(B * T, ret_dim),
                               at_out.reshape(B * T, att_dim), *wo, p_all, col_gate, tm=1088)
            mo = matmul(m, w_out[l].astype(BF16)).reshape(B, T, D)
            h_ctx = h_ctx + cg1 * mo[:, :NC]
            x = x + g1 * mo[:, NC:]
            h_all = jnp.concatenate([h_ctx, x], axis=1)
            n2 = jnp.concatenate([_rms_norm(h_ctx, norm2_w[l]) * (1.0 + csc2) + csh2,
                                  _rms_norm(x, norm2_w[l]) * (1.0 + sc2) + sh2], axis=1)
            f = conv_ffn(n2, l, seg_start, seg_end)
            h_ctx = h_ctx + cg2 * f[:, :NC]
            x = x + g2 * f[:, NC:]
        else:
            p_x = p3[:, NC:].reshape(B * S, -1)
            m = merge_branches(rw_out[:, NC:].reshape(B * S, rw_dim), rt_out[:, NC:].reshape(B * S, ret_dim),
                               at_out.reshape(B * S, att_dim), *wo, p_x, col_gate, tm=1024)
            mo = matmul(m, w_out[l].astype(BF16)).reshape(B, S, D)
            x = x + g1 * mo
            n2 = _rms_norm(x, norm2_w[l]) * (1.0 + sc2) + sh2
            xs = jnp.arange(S)
            f = conv_ffn(n2, l, (xs == 0).astype(F32), (xs == S - 1).astype(F32))
            x = x + g2 * f
    return x
```

```python
import functools

import jax
import jax.numpy as jnp
from jax import lax
from jax.experimental import pallas as pl
from jax.experimental.pallas import tpu as pltpu

F32 = jnp.float32
BF16 = jnp.bfloat16

V7X_VMEM_LIMIT_CAP = 56 * 1024 * 1024
LANE = 128
BF16_SUBLANES = 16

GRID_W = 64
RWKV_HEAD = 64
RWKV_LN_EPS = 64e-5
RWKV_CHUNK = 64
RET_V_HEAD = 256
RET_CHUNK = 128
RET_NORM_EPS = 1e-6
ATT_HEAD = 128
ATT_GROUP = 4
ROPE_THETA = 10000.0
NORM_EPS = 1e-6

_NT = (((1,), (1,)), ((), ()))
_TN = (((0,), (0,)), ((), ()))


def _vmem_limit(tile_bytes):
    return int(min(max(2 * tile_bytes + (8 << 20), 16 << 20), V7X_VMEM_LIMIT_CAP))


def _dot(x, y, dims=(((1,), (0,)), ((), ()))):
    return lax.dot_general(x.astype(BF16), y.astype(BF16), dims, preferred_element_type=F32)


def _mm_kernel(a_ref, b_ref, o_ref, *, nk):
    part = _dot(a_ref[...], b_ref[...])
    if nk == 1:
        o_ref[...] = part.astype(o_ref.dtype)
    else:
        k = pl.program_id(2)

        @pl.when(k == 0)
        def _():
            o_ref[...] = part

        @pl.when(k > 0)
        def _():
            o_ref[...] += part


def _pick(n, prefs):
    for p in prefs:
        if n % p == 0:
            return p
    return n


def matmul(a, b, *, layer=None, rows=None, tm=None, tn=None, tk=None, out_dtype=F32):
    M, K = a.shape
    M = rows or M
    N = b.shape[-1]
    assert b.shape[-2] == K
    tm = tm or _pick(M, (1088, 1024, 512, 256, 128, 8))
    tn = tn or _pick(N, (1024, 640, 512, 256, 128))
    tk = tk or K
    assert M % tm == 0 and N % tn == 0 and K % tk == 0, (M, N, K, tm, tn, tk)
    nk = K // tk
    if nk > 1:
        assert out_dtype == F32
    if b.ndim == 3:
        b_spec = pl.BlockSpec((None, tk, tn), lambda i, j, k: (layer, k, j))
    else:
        b_spec = pl.BlockSpec((tk, tn), lambda i, j, k: (k, j))
    tile_bytes = (tm * tk * a.dtype.itemsize + tk * tn * (b.dtype.itemsize + 2)
                  + tm * tn * (jnp.dtype(out_dtype).itemsize + 4))
    return pl.pallas_call(
        functools.partial(_mm_kernel, nk=nk),
        grid=(M // tm, N // tn, nk),
        in_specs=[pl.BlockSpec((tm, tk), lambda i, j, k: (i, k)), b_spec],
        out_specs=pl.BlockSpec((tm, tn), lambda i, j, k: (i, j)),
        out_shape=jax.ShapeDtypeStruct((M, N), out_dtype),
        compiler_params=pltpu.CompilerParams(
            dimension_semantics=("parallel", "parallel", "arbitrary"),
            vmem_limit_bytes=_vmem_limit(tile_bytes)),
        name="matmul",
    )(a, b)


def _ffn_up_kernel(a_ref, ap_ref, an_ref, wv_ref, wg_ref, cv_ref, cg_ref, bv_ref, bg_ref, o_ref, ext_ref,
                   *, tm, halo, n_x, seq, n_ctx):
    i = pl.program_id(0)
    j = pl.program_id(1)

    @pl.when(j == 0)
    def _():
        ext_ref[0:halo, :] = ap_ref[...]
        ext_ref[halo:halo + tm, :] = a_ref[...]
        ext_ref[halo + tm:, :] = an_ref[...]

    g = i * tm + lax.broadcasted_iota(jnp.int32, (tm, 1), 0)
    in_x = g < n_x
    starts = jnp.where(in_x, g & (seq - 1), (g - n_x) & (n_ctx - 1)) == 0
    ends = jnp.where(in_x, (g + 1) & (seq - 1), (g + 1 - n_x) & (n_ctx - 1)) == 0
    a_ext = ext_ref[...]

    def conv(w_ref, c_ref, b_ref):
        u = jnp.dot(a_ext, w_ref[...], preferred_element_type=F32)
        prev = jnp.where(starts, 0.0, u[halo - 1:halo - 1 + tm])
        nxt = jnp.where(ends, 0.0, u[halo + 1:halo + 1 + tm])
        c = c_ref[...]
        return prev * c[0:1] + u[halo:halo + tm] * c[1:2] + nxt * c[2:3] + b_ref[...]

    val = conv(wv_ref, cv_ref, bv_ref)
    gate = conv(wg_ref, cg_ref, bg_ref)
    o_ref[...] = (jax.nn.silu(gate) * val).astype(o_ref.dtype)


def ffn_up_conv_gate(a, w_up, conv_w, conv_b, *, layer, n_x, seq, n_ctx, tm, tn=512):
    R, D = a.shape
    F = w_up.shape[-1] // 2
    halo = BF16_SUBLANES
    assert R % tm == 0 and tm % halo == 0 and F % tn == 0
    assert seq & (seq - 1) == 0 and n_ctx & (n_ctx - 1) == 0
    nf = F // tn
    nh = tm // halo
    last = R // halo - 1
    tile_bytes = (tm * D * 2 + 2 * D * tn * 2 + tm * tn * 2) + ((tm + 2 * halo) * D * 2) // 2 + 4 * tm * tn * 4
    kern = functools.partial(_ffn_up_kernel, tm=tm, halo=halo, n_x=n_x, seq=seq, n_ctx=n_ctx)
    return pl.pallas_call(
        kern,
        grid=(R // tm, nf),
        in_specs=[pl.BlockSpec((tm, D), lambda i, j: (i, 0)),
                  pl.BlockSpec((halo, D), lambda i, j: (jnp.maximum(i * nh - 1, 0), 0)),
                  pl.BlockSpec((halo, D), lambda i, j: (jnp.minimum((i + 1) * nh, last), 0)),
                  pl.BlockSpec((None, D, tn), lambda i, j: (layer, 0, j)),
                  pl.BlockSpec((None, D, tn), lambda i, j: (layer, 0, nf + j)),
                  pl.BlockSpec((None, 3, tn), lambda i, j: (layer, 0, j)),
                  pl.BlockSpec((None, 3, tn), lambda i, j: (layer, 0, nf + j)),
                  pl.BlockSpec((None, 1, tn), lambda i, j: (layer, 0, j)),
                  pl.BlockSpec((None, 1, tn), lambda i, j: (layer, 0, nf + j))],
        out_specs=pl.BlockSpec((tm, tn), lambda i, j: (i, j)),
        out_shape=jax.ShapeDtypeStruct((R, F), BF16),
        scratch_shapes=[pltpu.VMEM((tm + 2 * halo, D), BF16)],
        compiler_params=pltpu.CompilerParams(
            dimension_semantics=("parallel", "arbitrary"),
            vmem_limit_bytes=_vmem_limit(tile_bytes)),
        name="ffn_up_conv_gate",
    )(a, a, a, w_up, w_up, conv_w, conv_w, conv_b, conv_b)


def _merge_kernel(rw_ref, rt_ref, at_ref, w1_ref, w2_ref, w3_ref, g1_ref, g2_ref, g3_ref, o_ref):
    def br(x_ref, w_ref, g_ref):
        return jax.nn.sigmoid(g_ref[...]) * _dot(x_ref[...], w_ref[...])

    m = br(rw_ref, w1_ref, g1_ref) + br(rt_ref, w2_ref, g2_ref) + br(at_ref, w3_ref, g3_ref)
    o_ref[...] = m.astype(o_ref.dtype)


def merge_branches(rw, rt, at, w1, w2, w3, p_all, gate_col0, *, layer, rows, tm, tn=512):
    D = w1.shape[-1]
    assert rows % tm == 0 and D % tn == 0 and gate_col0 % tn == 0
    g0 = gate_col0 // tn
    nd = D // tn
    k1, k2, k3 = rw.shape[1], rt.shape[1], at.shape[1]
    ks = k1 + k2 + k3
    tile_bytes = tm * ks * 2 + ks * tn * 2 + 3 * tm * tn * 4 + tm * tn * 2 + 3 * tm * tn * 4

    def gspec(i):
        return pl.BlockSpec((tm, tn), lambda m, n, i=i: (m, g0 + i * nd + n))

    def wspec(k):
        return pl.BlockSpec((None, k, tn), lambda m, n: (layer, 0, n))

    return pl.pallas_call(
        _merge_kernel,
        grid=(rows // tm, nd),
        in_specs=[pl.BlockSpec((tm, k1), lambda m, n: (m, 0)),
                  pl.BlockSpec((tm, k2), lambda m, n: (m, 0)),
                  pl.BlockSpec((tm, k3), lambda m, n: (m, 0)),
                  wspec(k1), wspec(k2), wspec(k3),
                  gspec(0), gspec(1), gspec(2)],
        out_specs=pl.BlockSpec((tm, tn), lambda m, n: (m, n)),
        out_shape=jax.ShapeDtypeStruct((rows, D), BF16),
        compiler_params=pltpu.CompilerParams(
            dimension_semantics=("parallel", "parallel"),
            vmem_limit_bytes=_vmem_limit(tile_bytes)),
        name="merge_branches",
    )(rw, rt, at, w1, w2, w3, p_all, p_all, p_all)


def _row_block_fn(n_batch, seq, n_ctx, chunk):
    nctx = n_ctx // chunk
    nx = seq // chunk
    coff = n_batch * nx

    def rb(b, d, c):
        fwd = jnp.where(c < nctx, coff + b * nctx + c, b * nx + (c - nctx))
        bwd = jnp.where(c < nctx, coff + b * nctx + (nctx - 1 - c), b * nx + (nctx + nx - 1 - c))
        return jnp.where(d == 0, fwd, bwd)

    return rb, nctx + nx


def _rwkv_kernel(r_ref, v_ref, kk_ref, lw_ref, kd_ref, as_ref, y_ref, s_ref, *, C, npair):
    fwd = pl.program_id(1) == 0

    @pl.when(pl.program_id(2) == 0)
    def _():
        s_ref[...] = jnp.zeros_like(s_ref)

    W = 2 * C
    row = lax.broadcasted_iota(jnp.int32, (C, W), 0)
    lane = lax.broadcasted_iota(jnp.int32, (C, W), 1)
    col = lane & (C - 1)
    ahead = jnp.where(fwd, row - col, col - row)
    strict = ahead > 0
    lower = ahead >= 0
    left = lane < C
    eye = jnp.where(ahead == 0, 1.0, 0.0)
    r0 = lax.broadcasted_iota(jnp.int32, (C, C), 0)
    c0 = lax.broadcasted_iota(jnp.int32, (C, C), 1)
    tri = jnp.where(jnp.where(fwd, r0 - c0, c0 - r0) >= 0, 1.0, 0.0).astype(BF16)
    bd_mask = (lax.broadcasted_iota(jnp.int32, (W, W), 0) < C) == (lax.broadcasted_iota(jnp.int32, (W, W), 1) < C)

    def only(x, first):
        return jnp.where(left, x, 0.0) if first else jnp.where(left, 0.0, x)

    def stack(*xs):
        return jnp.concatenate(xs, axis=0)

    def bdiag(p):
        return stack(only(p, True), only(p, False))

    P = range(npair)
    sls = [slice(p * W, (p + 1) * W) for p in P]
    r = [r_ref[:, sl] for sl in sls]
    v = [v_ref[:, sl] for sl in sls]
    kk = [kk_ref[:, sl] for sl in sls]
    lw = [lw_ref[:, sl] for sl in sls]
    kd = [kd_ref[:, sl] for sl in sls]
    b = [kk[p] * as_ref[:, sls[p]] for p in P]
    s = [s_ref[p] for p in P]

    def cumsum(x):
        l1 = x.astype(BF16)
        e1 = x - l1.astype(F32)
        l2 = e1.astype(BF16)
        l3 = (e1 - l2.astype(F32)).astype(BF16)
        g3 = _dot(tri, jnp.concatenate([l1, l2, l3], axis=1))
        return g3[:, :W] + g3[:, W:2 * W] + g3[:, 2 * W:]

    g_in = [cumsum(lw[p]) for p in P]
    g_last = [jnp.where(fwd, g[C - 1:C, :], g[0:1, :]) for g in g_in]
    e_neg = [jnp.exp(-g) for g in g_in]
    a_t = [-kk[p] * jnp.exp(g_in[p] - lw[p]) for p in P]
    r_t = [r[p] * jnp.exp(g_in[p]) for p in P]
    b_t = [b[p] * e_neg[p] for p in P]
    k_t = [kd[p] * e_neg[p] for p in P]
    e_rem = [jnp.exp(g_last[p] - g_in[p]) for p in P]
    b_h = [b[p] * e_rem[p] for p in P]
    k_h = [kd[p] * e_rem[p] for p in P]

    res_a = [_dot(stack(only(a_t[p], True), only(r_t[p], True)), stack(k_t[p], b_t[p]), _NT) for p in P]
    res_b = [_dot(stack(only(a_t[p], False), only(r_t[p], False)), stack(b_t[p], k_t[p]), _NT) for p in P]
    ak = [jnp.where(strict, jnp.where(left, res_a[p][:C], res_b[p][:C]), 0.0) for p in P]
    ab = [jnp.where(strict, jnp.where(left, res_b[p][:C], res_a[p][:C]), 0.0) for p in P]
    rk = [jnp.where(lower, jnp.where(left, res_a[p][C:], res_b[p][C:]), 0.0) for p in P]
    rb = [jnp.where(lower, jnp.where(left, res_b[p][C:], res_a[p][C:]), 0.0) for p in P]

    inv = [eye + ab[p] for p in P]
    pw = [_dot(ab[p], bdiag(ab[p])) for p in P]
    n = 2
    while n < C:
        both = [_dot(stack(inv[p], pw[p]), bdiag(pw[p])) for p in P]
        inv = [inv[p] + both[p][:C] for p in P]
        pw = [both[p][C:] for p in P]
        n *= 2

    a_s = [_dot(stack(a_t[p], r_t[p]), s[p], _NT) for p in P]
    inner = [a_s[p][:C] + _dot(ak[p], stack(only(v[p], True), only(v[p], False))) for p in P]
    u = [_dot(inv[p], stack(only(inner[p], False), only(inner[p], True))) for p in P]
    for p in P:
        y_ref[:, sls[p]] = a_s[p][C:] + _dot(
            jnp.concatenate([rb[p], rk[p]], axis=1),
            stack(only(u[p], False), only(u[p], True), only(v[p], True), only(v[p], False)))
    for p in P:
        upd = _dot(stack(u[p], v[p]), stack(b_h[p], k_h[p]), _TN)
        s_ref[p] = s[p] * jnp.exp(g_last[p]) + jnp.where(bd_mask, upd, 0.0)


def rwkv_scan(r, v, kk, lw, kd, a_sig, *, n_batch, seq, n_ctx, C=RWKV_CHUNK):
    R, HN = r.shape
    npair = HN // (2 * RWKV_HEAD)
    rb, steps = _row_block_fn(n_batch, seq, n_ctx, C)
    shared = pl.BlockSpec((C, HN), lambda b, d, c: (rb(b, d, c), 0))
    per_dir = pl.BlockSpec((None, C, HN), lambda b, d, c: (d, rb(b, d, c), 0))
    return pl.pallas_call(
        functools.partial(_rwkv_kernel, C=C, npair=npair),
        grid=(n_batch, 2, steps),
        in_specs=[shared, shared, shared, per_dir, per_dir, per_dir],
        out_specs=per_dir,
        out_shape=jax.ShapeDtypeStruct((2, R, HN), F32),
        scratch_shapes=[pltpu.VMEM((npair, 2 * RWKV_HEAD, 2 * RWKV_HEAD), F32)],
        compiler_params=pltpu.CompilerParams(dimension_semantics=("parallel", "parallel", "arbitrary")),
        name="rwkv_scan",
    )(r, v, kk, lw, kd, a_sig)


def _ret_kernel(lg_ref, q_ref, k_ref, v_ref, o_ref, s_ref, *, C):
    fwd = pl.program_id(1) == 0

    @pl.when(pl.program_id(3) == 0)
    def _():
        s_ref[...] = jnp.zeros_like(s_ref)

    lg = lg_ref[:, 0:1]
    q = q_ref[...]
    k = k_ref[...]
    v = v_ref[...]
    s = s_ref[...]
    row = lax.broadcasted_iota(jnp.int32, (C, C), 0)
    col = lax.broadcasted_iota(jnp.int32, (C, C), 1)
    ahead = jnp.where(fwd, row - col, col - row)
    inner = jnp.where(ahead >= 0, jnp.exp(jnp.maximum(ahead, 0).astype(F32) * lg), 0.0)
    idx = lax.broadcasted_iota(jnp.int32, (C, 1), 0)
    pos = jnp.where(fwd, idx, C - 1 - idx).astype(F32)
    q_decay = jnp.exp((pos + 1.0) * lg)
    k_decay = jnp.exp((C - 1.0 - pos) * lg)
    chunk_decay = jnp.exp(C * lg)

    scores = _dot(q, k, _NT) * inner
    o_ref[...] = _dot(scores, v) + _dot(q * q_decay, s)
    s_ref[...] = s * chunk_decay + _dot(k * k_decay, v, _TN)


def retention_scan(q, k, v, lg, *, n_batch, seq, n_ctx, C=RET_CHUNK):
    R = q.shape[0]
    dv = RET_V_HEAD
    dk = dv // 2
    H = v.shape[1] // dv
    rb, steps = _row_block_fn(n_batch, seq, n_ctx, C)
    lg_b = jnp.broadcast_to(lg.astype(F32).reshape(2 * H, 1, 1), (2 * H, 1, LANE))
    return pl.pallas_call(
        functools.partial(_ret_kernel, C=C),
        grid=(n_batch, 2, H, steps),
        in_specs=[pl.BlockSpec((None, 1, LANE), lambda b, d, h, c: (d * H + h, 0, 0)),
                  pl.BlockSpec((C, dk), lambda b, d, h, c: (rb(b, d, c), h)),
                  pl.BlockSpec((C, dk), lambda b, d, h, c: (rb(b, d, c), h)),
                  pl.BlockSpec((C, dv), lambda b, d, h, c: (rb(b, d, c), h))],
        out_specs=pl.BlockSpec((None, C, dv), lambda b, d, h, c: (d, rb(b, d, c), h)),
        out_shape=jax.ShapeDtypeStruct((2, R, H * dv), F32),
        scratch_shapes=[pltpu.VMEM((dk, dv), F32)],
        compiler_params=pltpu.CompilerParams(
            dimension_semantics=("parallel", "parallel", "parallel", "arbitrary")),
        name="retention_scan",
    )(lg_b, q, k, v)


def _attn_kernel(q_ref, *refs, nseg, group, dh):
    k_refs, v_refs, o_ref = refs[:nseg], refs[nseg:2 * nseg], refs[2 * nseg]
    ks = [r[...] for r in k_refs]
    vs = [r[...] for r in v_refs]
    for g in range(group):
        q = q_ref[:, g * dh:(g + 1) * dh]
        ss = [lax.dot_general(q, k, _NT, preferred_element_type=F32) for k in ks]
        m = functools.reduce(jnp.maximum, [jnp.max(s, axis=-1, keepdims=True) for s in ss])
        ps = [jnp.exp(s - m) for s in ss]
        l = functools.reduce(jnp.add, [jnp.sum(p, axis=-1, keepdims=True) for p in ps])
        o = functools.reduce(jnp.add, [jnp.dot(p.astype(BF16), v, preferred_element_type=F32)
                                       for p, v in zip(ps, vs)])
        o_ref[:, g * dh:(g + 1) * dh] = (o / l).astype(o_ref.dtype)


def gqa_attention(q, k, v, *, n_batch, q_row0, q_len, segs, tq=256, group=ATT_GROUP, dh=ATT_HEAD):
    hkv = k.shape[1] // dh
    gw = group * dh
    assert q_len % tq == 0 and q_row0 % tq == 0
    nq = q_len // tq
    q0 = q_row0 // tq
    for row0, ln in segs:
        assert row0 % ln == 0
    kv_specs = [pl.BlockSpec((ln, dh), lambda b, h, i, row0=row0, ln=ln: (row0 // ln + b, h))
                for row0, ln in segs]
    tk = sum(ln for _, ln in segs)
    tile_bytes = tq * gw * 2 * 2 + 2 * tk * dh * 2 + 3 * tq * tk * 4
    return pl.pallas_call(
        functools.partial(_attn_kernel, nseg=len(segs), group=group, dh=dh),
        grid=(n_batch, hkv, nq),
        in_specs=[pl.BlockSpec((tq, gw), lambda b, h, i: (q0 + b * nq + i, h))] + kv_specs + kv_specs,
        out_specs=pl.BlockSpec((tq, gw), lambda b, h, i: (b * nq + i, h)),
        out_shape=jax.ShapeDtypeStruct((n_batch * q_len, hkv * gw), BF16),
        compiler_params=pltpu.CompilerParams(
            dimension_semantics=("parallel", "parallel", "parallel"),
            vmem_limit_bytes=_vmem_limit(tile_bytes)),
        name="gqa_attention",
    )(q, *([k] * len(segs)), *([v] * len(segs)))


def _rms_norm(x, w, eps=NORM_EPS):
    return x * lax.rsqrt(jnp.mean(x * x, axis=-1, keepdims=True) + eps) * w


def _head_ln(y, eps):
    yc = y - jnp.mean(y, axis=-1, keepdims=True)
    return yc * lax.rsqrt(jnp.mean(yc * yc, axis=-1, keepdims=True) + eps)


def _rope_angles(pos, dim):
    inv_freq = ROPE_THETA ** (-jnp.arange(0, dim, 2, dtype=F32) / dim)
    return pos.astype(F32)[:, None] * inv_freq[None, :]


def _rope(x, cos, sin):
    x1, x2 = jnp.split(x, 2, axis=-1)
    c = cos[:, None, :]
    s = sin[:, None, :]
    return jnp.concatenate([x1 * c - x2 * s, x2 * c + x1 * s], axis=-1)


def _shift_prev(h, starts):
    return jnp.pad(h, ((1, 0), (0, 0)))[:-1] * (1.0 - starts)[:, None]


def _shift_next(h, ends):
    return jnp.pad(h, ((0, 1), (0, 0)))[1:] * (1.0 - ends)[:, None]


def kernel(x, c, ctx, c_ctx, w_ada, b_ada, norm1_w, norm2_w, w_in, rwkv_mu_prev, rwkv_mu_next, rwkv_w0, rwkv_w2, rwkv_a0, rwkv_a2, rwkv_g2, rwkv_k_k, rwkv_k_a, rwkv_r_k, rwkv_ln_w, rwkv_ln_b, w_rwkv_out, ret_log_decay, w_ret_out, attn_q_norm, attn_k_norm, w_attn_out, w_out, w_ffn_up, ffn_conv_w, ffn_conv_b, w_ffn_down):
    B, S, D = x.shape
    NC = ctx.shape[1]
    NX = B * S
    R = NX + B * NC
    depth = w_in.shape[0]
    rw_dim = w_rwkv_out.shape[1]
    rw_heads = rw_dim // RWKV_HEAD
    lora_w = rwkv_w2.shape[2]
    lora_a = rwkv_a2.shape[2]
    rw_proj = rwkv_mu_prev.shape[1]
    ret_dim = w_ret_out.shape[1]
    ret_heads = ret_dim // RET_V_HEAD
    ret_qk = RET_V_HEAD // 2
    nqk = ret_heads * ret_qk
    ret_proj = 2 * nqk + 2 * ret_dim
    att_dim = w_attn_out.shape[1]
    att_heads = att_dim // ATT_HEAD
    att_kv = att_heads // ATT_GROUP
    att_proj = att_dim + 2 * att_kv * ATT_HEAD
    mix_proj = rw_proj + ret_proj + att_proj
    ffn_dim = w_ffn_down.shape[1]

    rw_pad = -rw_proj % LANE
    col_ret = rw_proj + rw_pad
    col_att = col_ret + ret_proj
    gate_pad = -(col_att + att_proj) % 512
    col_gate = col_att + att_proj + gate_pad

    w_in_p = jnp.concatenate(
        [w_in[..., :rw_proj], jnp.zeros((depth, D, rw_pad), F32), w_in[..., rw_proj:mix_proj],
         jnp.zeros((depth, D, gate_pad), F32), w_in[..., mix_proj:]], axis=-1).astype(BF16)
    w_up_bf = w_ffn_up.astype(BF16)
    w_down_bf = w_ffn_down.astype(BF16)
    w_out_bf = w_out.astype(BF16)
    w_ro_bf, w_to_bf, w_ao_bf = w_rwkv_out.astype(BF16), w_ret_out.astype(BF16), w_attn_out.astype(BF16)
    w2_bf = rwkv_w2.reshape(depth * 2, lora_w, rw_dim).astype(BF16)
    a2_bf = rwkv_a2.reshape(depth * 2, lora_a, rw_dim).astype(BF16)
    g2_bf = rwkv_g2.astype(BF16)
    conv_b3 = ffn_conv_b[:, None, :]

    pos = jnp.arange(S)

    def table(ang):
        ang = jnp.concatenate([jnp.tile(ang, (B, 1)), jnp.zeros((B * NC, ang.shape[1]), F32)], axis=0)
        return jnp.cos(ang), jnp.sin(ang)

    cos_row, sin_row = table(_rope_angles(pos // GRID_W, ATT_HEAD // 2))
    cos_col, sin_col = table(_rope_angles(pos % GRID_W, ATT_HEAD // 2))
    cos_seq, sin_seq = table(_rope_angles(pos, ret_qk))

    rid = jnp.arange(R)
    seg_start = jnp.where(rid < NX, rid % S == 0, (rid - NX) % NC == 0).astype(F32)
    seg_end = jnp.where(rid < NX, (rid + 1) % S == 0, (rid + 1 - NX) % NC == 0).astype(F32)

    silu_all = jnp.zeros((8, D), F32).at[:B].set(jax.nn.silu(c)).at[B].set(jax.nn.silu(c_ctx))

    def modulate(h, w, scale, shift, n_rows):
        hx = _rms_norm(h[:NX].reshape(B, S, D), w) * (1.0 + scale[:B, None, :]) + shift[:B, None, :]
        parts = [hx.reshape(NX, D)]
        if n_rows > NX:
            parts.append(_rms_norm(h[NX:], w) * (1.0 + scale[B]) + shift[B])
        return jnp.concatenate(parts, axis=0).astype(BF16)

    def gated_add(h, gate, f, n_rows):
        hx = h[:NX].reshape(B, S, D) + gate[:B, None, :] * f[:NX].reshape(B, S, D)
        parts = [hx.reshape(NX, D)]
        if n_rows > NX:
            parts.append(h[NX:] + gate[B] * f[NX:])
        else:
            parts.append(h[NX:])
        return jnp.concatenate(parts, axis=0)

    h = jnp.concatenate([x.reshape(NX, D), ctx.reshape(B * NC, D)], axis=0)
    for l in range(depth):
        ctx_out = l < depth - 1
        n_rows = R if ctx_out else NX
        tm_rows = 1088 if ctx_out else 1024
        mod = matmul(silu_all, w_ada, layer=l, tm=8, tn=1024) + b_ada[l][None, :]
        sh1, sc1, g1, sh2, sc2, g2 = jnp.split(mod, 6, axis=-1)

        n1 = modulate(h, norm1_w[l], sc1, sh1, R)
        p_all = matmul(n1, w_in_p, layer=l, tn=512)

        p = p_all[:, :rw_proj]
        p = (p + rwkv_mu_prev[l] * (_shift_prev(p, seg_start) - p)
             + rwkv_mu_next[l] * (_shift_next(p, seg_end) - p))
        cuts = [rw_dim, 2 * rw_dim, 3 * rw_dim, 3 * rw_dim + lora_w, 3 * rw_dim + lora_w + lora_a]
        r_, k_, v_, w_low, a_low, g_low = jnp.split(p, cuts, axis=-1)

        def heads(t):
            return t.reshape(R, rw_heads, RWKV_HEAD)

        kk = heads(k_ * rwkv_k_k[l])
        kk = (kk / jnp.maximum(jnp.sqrt(jnp.sum(kk * kk, axis=-1, keepdims=True)), 1e-12)).reshape(R, rw_dim)
        tanh_w = jnp.tanh(w_low).astype(BF16)
        a_low_bf = a_low.astype(BF16)
        lws, kds, sigs = [], [], []
        bonus = 0.0
        for d in range(2):
            w_pre = rwkv_w0[l, d] + matmul(tanh_w, w2_bf, layer=2 * l + d)
            lws.append(-jnp.exp(-jax.nn.softplus(-w_pre) - 0.5))
            a_sig = jax.nn.sigmoid(rwkv_a0[l, d] + matmul(a_low_bf, a2_bf, layer=2 * l + d))
            kd = k_ * (1.0 + (a_sig - 1.0) * rwkv_k_a[l])
            bonus = bonus + jnp.sum(heads(r_) * heads(kd) * rwkv_r_k[l], axis=-1, keepdims=True) * heads(v_)
            kds.append(kd)
            sigs.append(a_sig)
        y2 = rwkv_scan(r_, v_, kk, jnp.stack(lws), jnp.stack(kds), jnp.stack(sigs),
                       n_batch=B, seq=S, n_ctx=NC)
        y_rw = _head_ln(heads(y2[0] + y2[1]), RWKV_LN_EPS).reshape(R, rw_dim) * rwkv_ln_w[l] + rwkv_ln_b[l]
        y_rw = y_rw + bonus.reshape(R, rw_dim)
        rw_out = (y_rw * matmul(jax.nn.sigmoid(g_low).astype(BF16), g2_bf, layer=l)).astype(BF16)

        pr = p_all[:, col_ret:col_ret + ret_proj]
        q_, k_r, v_r, g_r = jnp.split(pr, [nqk, 2 * nqk, 2 * nqk + ret_dim], axis=-1)
        q_ = _rope(q_.reshape(R, ret_heads, ret_qk), cos_seq, sin_seq).reshape(R, nqk)
        k_r = _rope(k_r.reshape(R, ret_heads, ret_qk) * (ret_qk ** -0.5), cos_seq, sin_seq).reshape(R, nqk)
        o2 = retention_scan(q_, k_r, v_r, ret_log_decay[l], n_batch=B, seq=S, n_ctx=NC)
        o_ret = _head_ln((o2[0] + o2[1]).reshape(R, ret_heads, RET_V_HEAD), RET_NORM_EPS).reshape(R, ret_dim)
        rt_out = (o_ret * jax.nn.silu(g_r)).astype(BF16)

        pa = p_all[:, col_att:col_att + att_proj]
        qa, ka, va = jnp.split(pa, [att_dim, att_dim + att_kv * ATT_HEAD], axis=-1)
        qa = _rms_norm(qa.reshape(R, att_heads, ATT_HEAD), attn_q_norm[l])
        ka = _rms_norm(ka.reshape(R, att_kv, ATT_HEAD), attn_k_norm[l])

        def axial(t):
            t_row, t_col = jnp.split(t, 2, axis=-1)
            return jnp.concatenate([_rope(t_row, cos_row, sin_row), _rope(t_col, cos_col, sin_col)], axis=-1)

        qa = (axial(qa) * (ATT_HEAD ** -0.5)).reshape(R, att_dim).astype(BF16)
        ka = axial(ka).reshape(R, att_kv * ATT_HEAD).astype(BF16)
        va = va.astype(BF16)
        at_out = gqa_attention(qa, ka, va, n_batch=B, q_row0=0, q_len=S, segs=((NX, NC), (0, S)))
        if ctx_out:
            at_c = gqa_attention(qa, ka, va, n_batch=B, q_row0=NX, q_len=NC, segs=((NX, NC),))
            at_out = jnp.concatenate([at_out, at_c], axis=0)

        m = merge_branches(rw_out, rt_out, at_out, w_ro_bf, w_to_bf, w_ao_bf, p_all, col_gate,
                           layer=l, rows=n_rows, tm=tm_rows)
        mo = matmul(m, w_out_bf, layer=l, tm=tm_rows)
        h = gated_add(h, g1, mo, n_rows)
        n2 = modulate(h, norm2_w[l], sc2, sh2, n_rows)
        act = ffn_up_conv_gate(n2, w_up_bf, ffn_conv_w, conv_b3, layer=l, n_x=NX, seq=S, n_ctx=NC, tm=tm_rows, tn=256)
        f = matmul(act, w_down_bf, layer=l, tm=tm_rows, tn=512, tk=ffn_dim // 2)
        h = gated_add(h, g2, f, n_rows)
    return h[:NX].reshape(B, S, D)
```

```python
import functools

import jax
import jax.numpy as jnp
from jax import lax
from jax.experimental import pallas as pl
from jax.experimental.pallas import tpu as pltpu

F32 = jnp.float32
BF16 = jnp.bfloat16

V7X_VMEM_LIMIT_CAP = 56 * 1024 * 1024
LANE = 128
BF16_SUBLANES = 16

GRID_W = 64
RWKV_HEAD = 64
RWKV_LN_EPS = 64e-5
RWKV_CHUNK = 64
RET_V_HEAD = 256
RET_CHUNK = 128
RET_NORM_EPS = 1e-6
ATT_HEAD = 128
ATT_GROUP = 4
ROPE_THETA = 10000.0
NORM_EPS = 1e-6

_NT = (((1,), (1,)), ((), ()))
_TN = (((0,), (0,)), ((), ()))


def _vmem_limit(tile_bytes):
    return int(min(max(2 * tile_bytes + (8 << 20), 16 << 20), V7X_VMEM_LIMIT_CAP))


def _dot(x, y, dims=(((1,), (0,)), ((), ()))):
    return lax.dot_general(x.astype(BF16), y.astype(BF16), dims, preferred_element_type=F32)


def _split_bf16(x):
    hi = x.astype(BF16)
    return hi, (x - hi.astype(F32)).astype(BF16)


def _dot3(x, y):
    xh, xl = _split_bf16(x)
    yh, yl = _split_bf16(y)
    return jnp.dot(jnp.concatenate([xh, xh, xl], axis=1), jnp.concatenate([yh, yl, yh], axis=0),
                   preferred_element_type=F32)


def _mm_kernel(a_ref, b_ref, o_ref, *, nk):
    part = _dot(a_ref[...], b_ref[...])
    if nk == 1:
        o_ref[...] = part.astype(o_ref.dtype)
    else:
        k = pl.program_id(2)

        @pl.when(k == 0)
        def _():
            o_ref[...] = part

        @pl.when(k > 0)
        def _():
            o_ref[...] += part


def _pick(n, prefs):
    for p in prefs:
        if n % p == 0:
            return p
    return n


def matmul(a, b, *, layer=None, rows=None, tm=None, tn=None, tk=None, out_dtype=F32):
    M, K = a.shape
    M = rows or M
    N = b.shape[-1]
    assert b.shape[-2] == K
    tm = tm or _pick(M, (1088, 1024, 512, 256, 128, 8))
    tn = tn or _pick(N, (1024, 640, 512, 256, 128))
    tk = tk or K
    assert M % tm == 0 and N % tn == 0 and K % tk == 0, (M, N, K, tm, tn, tk)
    nk = K // tk
    if nk > 1:
        assert out_dtype == F32
    if b.ndim == 3:
        b_spec = pl.BlockSpec((None, tk, tn), lambda i, j, k: (layer, k, j))
    else:
        b_spec = pl.BlockSpec((tk, tn), lambda i, j, k: (k, j))
    tile_bytes = (tm * tk * a.dtype.itemsize + tk * tn * (b.dtype.itemsize + 2)
                  + tm * tn * (jnp.dtype(out_dtype).itemsize + 4))
    return pl.pallas_call(
        functools.partial(_mm_kernel, nk=nk),
        grid=(M // tm, N // tn, nk),
        in_specs=[pl.BlockSpec((tm, tk), lambda i, j, k: (i, k)), b_spec],
        out_specs=pl.BlockSpec((tm, tn), lambda i, j, k: (i, j)),
        out_shape=jax.ShapeDtypeStruct((M, N), out_dtype),
        compiler_params=pltpu.CompilerParams(
            dimension_semantics=("parallel", "parallel", "arbitrary"),
            vmem_limit_bytes=_vmem_limit(tile_bytes)),
        name="matmul",
    )(a, b)


def _row_ids(row0, tm):
    return row0 + lax.broadcasted_iota(jnp.int32, (tm, 1), 0)


def _seq_edges(g, n_x, seq, n_ctx):
    in_x = g < n_x
    starts = jnp.where(in_x, g & (seq - 1), (g - n_x) & (n_ctx - 1)) == 0
    ends = jnp.where(in_x, (g + 1) & (seq - 1), (g + 1 - n_x) & (n_ctx - 1)) == 0
    return starts, ends


def _row_select(table, g, n_batch, seq):
    out = table[n_batch:n_batch + 1]
    seq_id = g >> (seq.bit_length() - 1)
    for b in range(n_batch):
        out = jnp.where(seq_id == b, table[b:b + 1], out)
    return out


def _head_sum(x, ones_bd):
    hi, lo = _split_bf16(x)
    return (jnp.dot(hi, ones_bd, preferred_element_type=F32)
            + jnp.dot(lo, ones_bd, preferred_element_type=F32))


def _mm_res_kernel(a_ref, b_ref, h_ref, g_ref, o_ref, *, nk, tm, n_batch, seq):
    part = _dot(a_ref[...], b_ref[...])
    k = pl.program_id(2)
    row0 = pl.program_id(0) * tm

    def finish(acc):
        gate = _row_select(g_ref[...], _row_ids(row0, tm), n_batch, seq)
        o_ref[...] = h_ref[...] + gate * acc

    if nk == 1:
        finish(part)
    else:
        @pl.when(k == 0)
        def _():
            o_ref[...] = part

        @pl.when((k > 0) & (k < nk - 1))
        def _():
            o_ref[...] += part

        @pl.when(k == nk - 1)
        def _():
            finish(o_ref[...] + part)


def matmul_residual(a, b, h, mod, chunk, *, layer, n_batch, seq, tm, tn, tk=None):
    M, K = a.shape
    N = b.shape[-1]
    tk = tk or K
    assert M % tm == 0 and N % tn == 0 and K % tk == 0
    nk = K // tk
    nn = N // tn
    tile_bytes = tm * tk * 2 + tk * tn * 2 + 3 * tm * tn * 4
    return pl.pallas_call(
        functools.partial(_mm_res_kernel, nk=nk, tm=tm, n_batch=n_batch, seq=seq),
        grid=(M // tm, nn, nk),
        in_specs=[pl.BlockSpec((tm, tk), lambda i, j, k: (i, k)),
                  pl.BlockSpec((None, tk, tn), lambda i, j, k: (layer, k, j)),
                  pl.BlockSpec((tm, tn), lambda i, j, k: (i, j)),
                  pl.BlockSpec((8, tn), lambda i, j, k: (0, chunk * nn + j))],
        out_specs=pl.BlockSpec((tm, tn), lambda i, j, k: (i, j)),
        out_shape=jax.ShapeDtypeStruct((M, N), F32),
        compiler_params=pltpu.CompilerParams(
            dimension_semantics=("parallel", "parallel", "arbitrary"),
            vmem_limit_bytes=_vmem_limit(tile_bytes)),
        name="matmul_residual",
    )(a, b, h, mod)


def _norm_mod_kernel(h_ref, w_ref, sc_ref, sh_ref, o_ref, *, tm, n_batch, seq):
    x = h_ref[...]
    g = _row_ids(pl.program_id(0) * tm, tm)
    y = x * lax.rsqrt(jnp.mean(x * x, axis=-1, keepdims=True) + NORM_EPS) * w_ref[...]
    y = y * (1.0 + _row_select(sc_ref[...], g, n_batch, seq)) + _row_select(sh_ref[...], g, n_batch, seq)
    o_ref[...] = y.astype(o_ref.dtype)


def norm_mod(h, norm_w, mod, shift_chunk, scale_chunk, *, layer, rows, n_batch, seq, tm=256):
    D = h.shape[1]
    assert rows % tm == 0
    return pl.pallas_call(
        functools.partial(_norm_mod_kernel, tm=tm, n_batch=n_batch, seq=seq),
        grid=(rows // tm,),
        in_specs=[pl.BlockSpec((tm, D), lambda i: (i, 0)),
                  pl.BlockSpec((None, 1, D), lambda i: (layer, 0, 0)),
                  pl.BlockSpec((8, D), lambda i: (0, scale_chunk)),
                  pl.BlockSpec((8, D), lambda i: (0, shift_chunk))],
        out_specs=pl.BlockSpec((tm, D), lambda i: (i, 0)),
        out_shape=jax.ShapeDtypeStruct((rows, D), BF16),
        compiler_params=pltpu.CompilerParams(
            dimension_semantics=("parallel",), vmem_limit_bytes=_vmem_limit(tm * D * 10)),
        name="norm_mod",
    )(h, norm_w[:, None, :], mod, mod)


def _ffn_up_kernel(a_ref, ap_ref, an_ref, wv_ref, wg_ref, cv_ref, cg_ref, bv_ref, bg_ref, o_ref, ext_ref,
                   *, tm, halo, n_x, seq, n_ctx):
    i = pl.program_id(0)
    j = pl.program_id(1)

    @pl.when(j == 0)
    def _():
        ext_ref[0:halo, :] = ap_ref[...]
        ext_ref[halo:halo + tm, :] = a_ref[...]
        ext_ref[halo + tm:, :] = an_ref[...]

    g = i * tm + lax.broadcasted_iota(jnp.int32, (tm, 1), 0)
    in_x = g < n_x
    starts = jnp.where(in_x, g & (seq - 1), (g - n_x) & (n_ctx - 1)) == 0
    ends = jnp.where(in_x, (g + 1) & (seq - 1), (g + 1 - n_x) & (n_ctx - 1)) == 0
    a_ext = ext_ref[...]

    def conv(w_ref, c_ref, b_ref):
        u = jnp.dot(a_ext, w_ref[...], preferred_element_type=F32)
        prev = jnp.where(starts, 0.0, u[halo - 1:halo - 1 + tm])
        nxt = jnp.where(ends, 0.0, u[halo + 1:halo + 1 + tm])
        c = c_ref[...]
        return prev * c[0:1] + u[halo:halo + tm] * c[1:2] + nxt * c[2:3] + b_ref[...]

    val = conv(wv_ref, cv_ref, bv_ref)
    gate = conv(wg_ref, cg_ref, bg_ref)
    o_ref[...] = (jax.nn.silu(gate) * val).astype(o_ref.dtype)


def ffn_up_conv_gate(a, w_up, conv_w, conv_b, *, layer, n_x, seq, n_ctx, tm, tn=512):
    R, D = a.shape
    F = w_up.shape[-1] // 2
    halo = BF16_SUBLANES
    assert R % tm == 0 and tm % halo == 0 and F % tn == 0
    assert seq & (seq - 1) == 0 and n_ctx & (n_ctx - 1) == 0
    nf = F // tn
    nh = tm // halo
    last = R // halo - 1
    tile_bytes = (tm * D * 2 + 2 * D * tn * 2 + tm * tn * 2) + ((tm + 2 * halo) * D * 2) // 2 + 4 * tm * tn * 4
    kern = functools.partial(_ffn_up_kernel, tm=tm, halo=halo, n_x=n_x, seq=seq, n_ctx=n_ctx)
    return pl.pallas_call(
        kern,
        grid=(R // tm, nf),
        in_specs=[pl.BlockSpec((tm, D), lambda i, j: (i, 0)),
                  pl.BlockSpec((halo, D), lambda i, j: (jnp.maximum(i * nh - 1, 0), 0)),
                  pl.BlockSpec((halo, D), lambda i, j: (jnp.minimum((i + 1) * nh, last), 0)),
                  pl.BlockSpec((None, D, tn), lambda i, j: (layer, 0, j)),
                  pl.BlockSpec((None, D, tn), lambda i, j: (layer, 0, nf + j)),
                  pl.BlockSpec((None, 3, tn), lambda i, j: (layer, 0, j)),
                  pl.BlockSpec((None, 3, tn), lambda i, j: (layer, 0, nf + j)),
                  pl.BlockSpec((None, 1, tn), lambda i, j: (layer, 0, j)),
                  pl.BlockSpec((None, 1, tn), lambda i, j: (layer, 0, nf + j))],
        out_specs=pl.BlockSpec((tm, tn), lambda i, j: (i, j)),
        out_shape=jax.ShapeDtypeStruct((R, F), BF16),
        scratch_shapes=[pltpu.VMEM((tm + 2 * halo, D), BF16)],
        compiler_params=pltpu.CompilerParams(
            dimension_semantics=("parallel", "arbitrary"),
            vmem_limit_bytes=_vmem_limit(tile_bytes)),
        name="ffn_up_conv_gate",
    )(a, a, a, w_up, w_up, conv_w, conv_w, conv_b, conv_b)


def _merge_kernel(rw_ref, rt_ref, at_ref, w1_ref, w2_ref, w3_ref, g1_ref, g2_ref, g3_ref, o_ref):
    def br(x_ref, w_ref, g_ref):
        return jax.nn.sigmoid(g_ref[...]) * _dot(x_ref[...], w_ref[...])

    m = br(rw_ref, w1_ref, g1_ref) + br(rt_ref, w2_ref, g2_ref) + br(at_ref, w3_ref, g3_ref)
    o_ref[...] = m.astype(o_ref.dtype)


def merge_branches(rw, rt, at, w1, w2, w3, p_all, gate_col0, *, layer, rows, tm, tn=512):
    D = w1.shape[-1]
    assert rows % tm == 0 and D % tn == 0 and gate_col0 % tn == 0
    g0 = gate_col0 // tn
    nd = D // tn
    k1, k2, k3 = rw.shape[1], rt.shape[1], at.shape[1]
    ks = k1 + k2 + k3
    tile_bytes = tm * ks * 2 + ks * tn * 2 + 3 * tm * tn * 4 + tm * tn * 2 + 3 * tm * tn * 4

    def gspec(i):
        return pl.BlockSpec((tm, tn), lambda m, n, i=i: (m, g0 + i * nd + n))

    def wspec(k):
        return pl.BlockSpec((None, k, tn), lambda m, n: (layer, 0, n))

    return pl.pallas_call(
        _merge_kernel,
        grid=(rows // tm, nd),
        in_specs=[pl.BlockSpec((tm, k1), lambda m, n: (m, 0)),
                  pl.BlockSpec((tm, k2), lambda m, n: (m, 0)),
                  pl.BlockSpec((tm, k3), lambda m, n: (m, 0)),
                  wspec(k1), wspec(k2), wspec(k3),
                  gspec(0), gspec(1), gspec(2)],
        out_specs=pl.BlockSpec((tm, tn), lambda m, n: (m, n)),
        out_shape=jax.ShapeDtypeStruct((rows, D), BF16),
        compiler_params=pltpu.CompilerParams(
            dimension_semantics=("parallel", "parallel"),
            vmem_limit_bytes=_vmem_limit(tile_bytes)),
        name="merge_branches",
    )(rw, rt, at, w1, w2, w3, p_all, p_all, p_all)


def _softplus(z):
    return jnp.maximum(z, 0.0) + jnp.log(1.0 + jnp.exp(-jnp.abs(z)))


def _rwkv_prep_kernel(p_ref, pp_ref, pn_ref, mup_ref, mun_ref, vec_ref, bias_ref, lora_ref, ones_ref,
                      r_ref, v_ref, kk_ref, lw_ref, kd_ref, as_ref, bonus_ref, gate_ref,
                      *, tm, dim, lora_w, lora_a, n_x, seq, n_ctx):
    halo = pp_ref.shape[0]
    g = _row_ids(pl.program_id(0) * tm, tm)
    starts, ends = _seq_edges(g, n_x, seq, n_ctx)
    p = p_ref[...]
    ext = jnp.concatenate([pp_ref[...], p, pn_ref[...]], axis=0)
    prev = jnp.where(starts, 0.0, ext[halo - 1:halo - 1 + tm])
    nxt = jnp.where(ends, 0.0, ext[halo + 1:halo + 1 + tm])
    x = p + mup_ref[...] * (prev - p) + mun_ref[...] * (nxt - p)

    r = x[:, :dim]
    k = x[:, dim:2 * dim]
    v = x[:, 2 * dim:3 * dim]
    tail = x[:, 3 * dim:]
    lane = lax.broadcasted_iota(jnp.int32, tail.shape, 1)
    act = jnp.where(lane < lora_w, jnp.tanh(tail), jnp.where(lane < lora_w + lora_a, tail, jax.nn.sigmoid(tail)))
    pre = _dot(act, lora_ref[...]) + bias_ref[...]

    vec = vec_ref[...]
    ones_bd = ones_ref[...]
    kk = k * vec[0:1]
    kk = kk * lax.rsqrt(jnp.maximum(_head_sum(kk * kk, ones_bd), 1e-24))
    r_ref[...] = r
    v_ref[...] = v
    kk_ref[...] = kk
    bonus = jnp.zeros_like(r)
    for d in range(2):
        w_pre = pre[:, 2 * d * dim:(2 * d + 1) * dim]
        a_pre = pre[:, (2 * d + 1) * dim:(2 * d + 2) * dim]
        lw_ref[d] = -jnp.exp(-_softplus(-w_pre) - 0.5)
        a_sig = jax.nn.sigmoid(a_pre)
        kd = k * (1.0 + (a_sig - 1.0) * vec[1:2])
        as_ref[d] = a_sig
        kd_ref[d] = kd
        bonus = bonus + _head_sum(r * kd * vec[2:3], ones_bd)
    bonus_ref[...] = bonus * v
    gate_ref[...] = pre[:, 4 * dim:]


def rwkv_prep(p_all, mu_prev, mu_next, vecs, bias, lora, ones_bd, *, layer, width, dim, lora_w, lora_a,
              n_x, seq, n_ctx, tm=128):
    R = p_all.shape[0]
    halo = 8
    assert R % tm == 0 and tm % halo == 0
    nh = tm // halo
    last = R // halo - 1
    row = pl.BlockSpec((tm, dim), lambda i: (i, 0))
    per_dir = pl.BlockSpec((2, tm, dim), lambda i: (0, i, 0))

    def par(a):
        return pl.BlockSpec((None,) + a.shape[1:], lambda i: (layer,) + (0,) * (a.ndim - 1))

    f = jax.ShapeDtypeStruct((R, dim), F32)
    f2 = jax.ShapeDtypeStruct((2, R, dim), F32)
    kern = functools.partial(_rwkv_prep_kernel, tm=tm, dim=dim, lora_w=lora_w, lora_a=lora_a,
                             n_x=n_x, seq=seq, n_ctx=n_ctx)
    return pl.pallas_call(
        kern,
        grid=(R // tm,),
        in_specs=[pl.BlockSpec((tm, width), lambda i: (i, 0)),
                  pl.BlockSpec((halo, width), lambda i: (jnp.maximum(i * nh - 1, 0), 0)),
                  pl.BlockSpec((halo, width), lambda i: (jnp.minimum((i + 1) * nh, last), 0)),
                  par(mu_prev), par(mu_next), par(vecs), par(bias), par(lora),
                  pl.BlockSpec(ones_bd.shape, lambda i: (0, 0))],
        out_specs=[row, row, row, per_dir, per_dir, per_dir, row, row],
        out_shape=[f, f, f, f2, f2, f2, f, f],
        compiler_params=pltpu.CompilerParams(
            dimension_semantics=("parallel",),
            vmem_limit_bytes=_vmem_limit(tm * (width + 20 * dim) * 4 + lora.shape[1] * lora.shape[2] * 2
                                         + dim * dim * 2)),
        name="rwkv_prep",
    )(p_all, p_all, p_all, mu_prev, mu_next, vecs, bias, lora, ones_bd)


def _rwkv_finish_kernel(y_ref, bonus_ref, gate_ref, ln_ref, ones_ref, o_ref, *, head):
    ones_bd = ones_ref[...]
    y = y_ref[0] + y_ref[1]
    yc = y - _head_sum(y, ones_bd) * (1.0 / head)
    var = _head_sum(yc * yc, ones_bd) * (1.0 / head)
    ln = ln_ref[...]
    out = yc * lax.rsqrt(var + RWKV_LN_EPS) * ln[0:1] + ln[1:2] + bonus_ref[...]
    o_ref[...] = (out * gate_ref[...]).astype(o_ref.dtype)


def rwkv_finish(y2, bonus, gate, ln_wb, ones_bd, *, layer, tm=256):
    _, R, dim = y2.shape
    assert R % tm == 0
    row = pl.BlockSpec((tm, dim), lambda i: (i, 0))
    return pl.pallas_call(
        functools.partial(_rwkv_finish_kernel, head=RWKV_HEAD),
        grid=(R // tm,),
        in_specs=[pl.BlockSpec((2, tm, dim), lambda i: (0, i, 0)), row, row,
                  pl.BlockSpec((None, 2, dim), lambda i: (layer, 0, 0)),
                  pl.BlockSpec(ones_bd.shape, lambda i: (0, 0))],
        out_specs=row,
        out_shape=jax.ShapeDtypeStruct((R, dim), BF16),
        compiler_params=pltpu.CompilerParams(
            dimension_semantics=("parallel",), vmem_limit_bytes=_vmem_limit(tm * dim * 40 + dim * dim * 2)),
        name="rwkv_finish",
    )(y2, bonus, gate, ln_wb, ones_bd)


def _row_block_fn(n_batch, seq, n_ctx, chunk):
    nctx = n_ctx // chunk
    nx = seq // chunk
    coff = n_batch * nx

    def rb(b, d, c):
        fwd = jnp.where(c < nctx, coff + b * nctx + c, b * nx + (c - nctx))
        bwd = jnp.where(c < nctx, coff + b * nctx + (nctx - 1 - c), b * nx + (nctx + nx - 1 - c))
        return jnp.where(d == 0, fwd, bwd)

    return rb, nctx + nx


def _rwkv_kernel(r_ref, v_ref, kk_ref, lw_ref, kd_ref, as_ref, y_ref, s_ref, *, C, npair):
    fwd = pl.program_id(1) == 0

    @pl.when(pl.program_id(2) == 0)
    def _():
        s_ref[...] = jnp.zeros_like(s_ref)

    W = 2 * C
    row = lax.broadcasted_iota(jnp.int32, (C, W), 0)
    lane = lax.broadcasted_iota(jnp.int32, (C, W), 1)
    col = lane & (C - 1)
    ahead = jnp.where(fwd, row - col, col - row)
    strict = ahead > 0
    lower = ahead >= 0
    left = lane < C
    eye = jnp.where(ahead == 0, 1.0, 0.0)
    r0 = lax.broadcasted_iota(jnp.int32, (C, C), 0)
    c0 = lax.broadcasted_iota(jnp.int32, (C, C), 1)
    tri = jnp.where(jnp.where(fwd, r0 - c0, c0 - r0) >= 0, 1.0, 0.0).astype(BF16)
    bd_mask = (lax.broadcasted_iota(jnp.int32, (W, W), 0) < C) == (lax.broadcasted_iota(jnp.int32, (W, W), 1) < C)

    def only(x, first):
        return jnp.where(left, x, 0.0) if first else jnp.where(left, 0.0, x)

    def stack(*xs):
        return jnp.concatenate(xs, axis=0)

    def bdiag(p):
        return stack(only(p, True), only(p, False))

    P = range(npair)
    sls = [slice(p * W, (p + 1) * W) for p in P]
    r = [r_ref[:, sl] for sl in sls]
    v = [v_ref[:, sl] for sl in sls]
    kk = [kk_ref[:, sl] for sl in sls]
    lw = [lw_ref[:, sl] for sl in sls]
    kd = [kd_ref[:, sl] for sl in sls]
    b = [kk[p] * as_ref[:, sls[p]] for p in P]
    s = [s_ref[p] for p in P]

    def cumsum(x):
        l1 = x.astype(BF16)
        e1 = x - l1.astype(F32)
        l2 = e1.astype(BF16)
        l3 = (e1 - l2.astype(F32)).astype(BF16)
        g3 = _dot(tri, jnp.concatenate([l1, l2, l3], axis=1))
        return g3[:, :W] + g3[:, W:2 * W] + g3[:, 2 * W:]

    g_in = [cumsum(lw[p]) for p in P]
    g_last = [jnp.where(fwd, g[C - 1:C, :], g[0:1, :]) for g in g_in]
    e_neg = [jnp.exp(-g) for g in g_in]
    a_t = [-kk[p] * jnp.exp(g_in[p] - lw[p]) for p in P]
    r_t = [r[p] * jnp.exp(g_in[p]) for p in P]
    b_t = [b[p] * e_neg[p] for p in P]
    k_t = [kd[p] * e_neg[p] for p in P]
    e_rem = [jnp.exp(g_last[p] - g_in[p]) for p in P]
    b_h = [b[p] * e_rem[p] for p in P]
    k_h = [kd[p] * e_rem[p] for p in P]

    res_a = [_dot(stack(only(a_t[p], True), only(r_t[p], True)), stack(k_t[p], b_t[p]), _NT) for p in P]
    res_b = [_dot(stack(only(a_t[p], False), only(r_t[p], False)), stack(b_t[p], k_t[p]), _NT) for p in P]
    ak = [jnp.where(strict, jnp.where(left, res_a[p][:C], res_b[p][:C]), 0.0) for p in P]
    ab = [jnp.where(strict, jnp.where(left, res_b[p][:C], res_a[p][:C]), 0.0) for p in P]
    rk = [jnp.where(lower, jnp.where(left, res_a[p][C:], res_b[p][C:]), 0.0) for p in P]
    rb = [jnp.where(lower, jnp.where(left, res_b[p][C:], res_a[p][C:]), 0.0) for p in P]

    inv = [eye + ab[p] for p in P]
    pw = [_dot3(ab[p], bdiag(ab[p])) for p in P]
    n = 2
    while n < C:
        both = [_dot3(stack(inv[p], pw[p]), bdiag(pw[p])) for p in P]
        inv = [inv[p] + both[p][:C] for p in P]
        pw = [both[p][C:] for p in P]
        n *= 2

    a_s = [_dot(stack(a_t[p], r_t[p]), s[p], _NT) for p in P]
    inner = [a_s[p][:C] + _dot(ak[p], stack(only(v[p], True), only(v[p], False))) for p in P]
    u = [_dot(inv[p], stack(only(inner[p], False), only(inner[p], True))) for p in P]
    for p in P:
        y_ref[:, sls[p]] = a_s[p][C:] + _dot(
            jnp.concatenate([rb[p], rk[p]], axis=1),
            stack(only(u[p], False), only(u[p], True), only(v[p], True), only(v[p], False)))
    for p in P:
        upd = _dot(stack(u[p], v[p]), stack(b_h[p], k_h[p]), _TN)
        s_ref[p] = s[p] * jnp.exp(g_last[p]) + jnp.where(bd_mask, upd, 0.0)


def rwkv_scan(r, v, kk, lw, kd, a_sig, *, n_batch, seq, n_ctx, C=RWKV_CHUNK):
    R, HN = r.shape
    npair = HN // (2 * RWKV_HEAD)
    rb, steps = _row_block_fn(n_batch, seq, n_ctx, C)
    shared = pl.BlockSpec((C, HN), lambda b, d, c: (rb(b, d, c), 0))
    per_dir = pl.BlockSpec((None, C, HN), lambda b, d, c: (d, rb(b, d, c), 0))
    return pl.pallas_call(
        functools.partial(_rwkv_kernel, C=C, npair=npair),
        grid=(n_batch, 2, steps),
        in_specs=[shared, shared, shared, per_dir, per_dir, per_dir],
        out_specs=per_dir,
        out_shape=jax.ShapeDtypeStruct((2, R, HN), F32),
        scratch_shapes=[pltpu.VMEM((npair, 2 * RWKV_HEAD, 2 * RWKV_HEAD), F32)],
        compiler_params=pltpu.CompilerParams(dimension_semantics=("parallel", "parallel", "arbitrary")),
        name="rwkv_scan",
    )(r, v, kk, lw, kd, a_sig)


def _ret_kernel(lg_ref, q_ref, k_ref, v_ref, cos_ref, sin_ref, o_ref, s_ref, *, C):
    fwd = pl.program_id(1) == 0

    @pl.when(pl.program_id(3) == 0)
    def _():
        s_ref[...] = jnp.zeros_like(s_ref)

    lg = lg_ref[:, 0:1]
    cos = cos_ref[...]
    sin = sin_ref[...]
    half = q_ref.shape[1] // 2
    q = q_ref[...]
    q = q * cos + pltpu.roll(q, half, 1) * sin
    k = k_ref[...] * (q_ref.shape[1] ** -0.5)
    k = k * cos + pltpu.roll(k, half, 1) * sin
    v = v_ref[...]
    s = s_ref[...]
    row = lax.broadcasted_iota(jnp.int32, (C, C), 0)
    col = lax.broadcasted_iota(jnp.int32, (C, C), 1)
    ahead = jnp.where(fwd, row - col, col - row)
    inner = jnp.where(ahead >= 0, jnp.exp(jnp.maximum(ahead, 0).astype(F32) * lg), 0.0)
    idx = lax.broadcasted_iota(jnp.int32, (C, 1), 0)
    pos = jnp.where(fwd, idx, C - 1 - idx).astype(F32)
    q_decay = jnp.exp((pos + 1.0) * lg)
    k_decay = jnp.exp((C - 1.0 - pos) * lg)
    chunk_decay = jnp.exp(C * lg)

    scores = _dot(q, k, _NT) * inner
    o_ref[...] = _dot(scores, v) + _dot(q * q_decay, s)
    s_ref[...] = s * chunk_decay + _dot(k * k_decay, v, _TN)


def retention_scan(p_all, col0, H, cos2, sin2, lg, *, n_batch, seq, n_ctx, C=RET_CHUNK):
    R = p_all.shape[0]
    dv = RET_V_HEAD
    dk = dv // 2
    assert col0 % dv == 0
    q0 = col0 // dk
    k0 = q0 + H
    v0 = (col0 + 2 * H * dk) // dv
    rb, steps = _row_block_fn(n_batch, seq, n_ctx, C)
    lg_b = jnp.broadcast_to(lg.astype(F32).reshape(2 * H, 1, 1), (2 * H, 1, LANE))
    tab = pl.BlockSpec((C, dk), lambda b, d, h, c: (rb(b, d, c), 0))
    return pl.pallas_call(
        functools.partial(_ret_kernel, C=C),
        grid=(n_batch, 2, H, steps),
        in_specs=[pl.BlockSpec((None, 1, LANE), lambda b, d, h, c: (d * H + h, 0, 0)),
                  pl.BlockSpec((C, dk), lambda b, d, h, c: (rb(b, d, c), q0 + h)),
                  pl.BlockSpec((C, dk), lambda b, d, h, c: (rb(b, d, c), k0 + h)),
                  pl.BlockSpec((C, dv), lambda b, d, h, c: (rb(b, d, c), v0 + h)),
                  tab, tab],
        out_specs=pl.BlockSpec((None, C, dv), lambda b, d, h, c: (d, rb(b, d, c), h)),
        out_shape=jax.ShapeDtypeStruct((2, R, H * dv), F32),
        scratch_shapes=[pltpu.VMEM((dk, dv), F32)],
        compiler_params=pltpu.CompilerParams(
            dimension_semantics=("parallel", "parallel", "parallel", "arbitrary")),
        name="retention_scan",
    )(lg_b, p_all, p_all, p_all, cos2, sin2)


def _ret_finish_kernel(o_ref, g_ref, out_ref):
    o = o_ref[0] + o_ref[1]
    oc = o - jnp.mean(o, axis=-1, keepdims=True)
    y = oc * lax.rsqrt(jnp.mean(oc * oc, axis=-1, keepdims=True) + RET_NORM_EPS)
    out_ref[...] = (y * jax.nn.silu(g_ref[...])).astype(out_ref.dtype)


def retention_finish(o2, p_all, gcol0, *, tm=512):
    _, R, HD = o2.shape
    dv = RET_V_HEAD
    assert R % tm == 0 and gcol0 % dv == 0
    g0 = gcol0 // dv
    return pl.pallas_call(
        _ret_finish_kernel,
        grid=(R // tm, HD // dv),
        in_specs=[pl.BlockSpec((2, tm, dv), lambda i, h: (0, i, h)),
                  pl.BlockSpec((tm, dv), lambda i, h: (i, g0 + h))],
        out_specs=pl.BlockSpec((tm, dv), lambda i, h: (i, h)),
        out_shape=jax.ShapeDtypeStruct((R, HD), BF16),
        compiler_params=pltpu.CompilerParams(dimension_semantics=("parallel", "parallel")),
        name="retention_finish",
    )(o2, p_all)


def _norm_rope(x, w, cos, sin):
    quarter = x.shape[1] // 4
    y = x * lax.rsqrt(jnp.mean(x * x, axis=-1, keepdims=True) + NORM_EPS) * w
    lane = lax.broadcasted_iota(jnp.int32, y.shape, 1)
    first = (lane & (2 * quarter - 1)) < quarter
    swapped = jnp.where(first, pltpu.roll(y, 3 * quarter, 1), pltpu.roll(y, quarter, 1))
    return y * cos + swapped * sin


def _kv_prep_kernel(k_ref, v_ref, w_ref, cos_ref, sin_ref, ko_ref, vo_ref, *, dh):
    cos = cos_ref[...]
    sin = sin_ref[...]
    w = w_ref[...]
    for h in range(k_ref.shape[1] // dh):
        sl = slice(h * dh, (h + 1) * dh)
        ko_ref[:, sl] = _norm_rope(k_ref[:, sl], w, cos, sin).astype(ko_ref.dtype)
    vo_ref[...] = v_ref[...].astype(vo_ref.dtype)


def attention_kv_prep(p_all, kcol0, width, k_norm, cos, sin, *, layer, tm=512, dh=ATT_HEAD):
    R = p_all.shape[0]
    assert R % tm == 0 and kcol0 % width == 0
    kb = kcol0 // width
    out = jax.ShapeDtypeStruct((R, width), BF16)
    tab = pl.BlockSpec((tm, dh), lambda i: (i, 0))
    return pl.pallas_call(
        functools.partial(_kv_prep_kernel, dh=dh),
        grid=(R // tm,),
        in_specs=[pl.BlockSpec((tm, width), lambda i: (i, kb)),
                  pl.BlockSpec((tm, width), lambda i: (i, kb + 1)),
                  pl.BlockSpec((None, 1, dh), lambda i: (layer, 0, 0)), tab, tab],
        out_specs=[pl.BlockSpec((tm, width), lambda i: (i, 0))] * 2,
        out_shape=[out, out],
        compiler_params=pltpu.CompilerParams(dimension_semantics=("parallel",)),
        name="attention_kv_prep",
    )(p_all, p_all, k_norm[:, None, :], cos, sin)


def _attn_kernel(q_ref, w_ref, cos_ref, sin_ref, *refs, nseg, group, dh):
    k_refs, v_refs, o_ref = refs[:nseg], refs[nseg:2 * nseg], refs[2 * nseg]
    ks = [r[...] for r in k_refs]
    vs = [r[...] for r in v_refs]
    cos = cos_ref[...]
    sin = sin_ref[...]
    w = w_ref[...] * (dh ** -0.5)
    for g in range(group):
        q = _norm_rope(q_ref[:, g * dh:(g + 1) * dh], w, cos, sin).astype(BF16)
        ss = [lax.dot_general(q, k, _NT, preferred_element_type=F32) for k in ks]
        m = functools.reduce(jnp.maximum, [jnp.max(s, axis=-1, keepdims=True) for s in ss])
        ps = [jnp.exp(s - m) for s in ss]
        l = functools.reduce(jnp.add, [jnp.sum(p, axis=-1, keepdims=True) for p in ps])
        o = functools.reduce(jnp.add, [jnp.dot(p.astype(BF16), v, preferred_element_type=F32)
                                       for p, v in zip(ps, vs)])
        o_ref[:, g * dh:(g + 1) * dh] = (o / l).astype(o_ref.dtype)


def gqa_attention(p_all, qcol0, q_norm, cos, sin, k, v, *, layer, n_batch, q_row0, q_len, segs, tq=256,
                  group=ATT_GROUP, dh=ATT_HEAD):
    hkv = k.shape[1] // dh
    gw = group * dh
    assert q_len % tq == 0 and q_row0 % tq == 0 and qcol0 % gw == 0
    nq = q_len // tq
    q0 = q_row0 // tq
    qc = qcol0 // gw
    for row0, ln in segs:
        assert row0 % ln == 0
    kv_specs = [pl.BlockSpec((ln, dh), lambda b, h, i, row0=row0, ln=ln: (row0 // ln + b, h))
                for row0, ln in segs]
    tk = sum(ln for _, ln in segs)
    tile_bytes = tq * gw * 6 + 2 * tk * dh * 2 + 3 * tq * tk * 4
    tab = pl.BlockSpec((tq, dh), lambda b, h, i: (q0 + b * nq + i, 0))
    return pl.pallas_call(
        functools.partial(_attn_kernel, nseg=len(segs), group=group, dh=dh),
        grid=(n_batch, hkv, nq),
        in_specs=[pl.BlockSpec((tq, gw), lambda b, h, i: (q0 + b * nq + i, qc + h)),
                  pl.BlockSpec((None, 1, dh), lambda b, h, i: (layer, 0, 0)), tab, tab] + kv_specs + kv_specs,
        out_specs=pl.BlockSpec((tq, gw), lambda b, h, i: (b * nq + i, h)),
        out_shape=jax.ShapeDtypeStruct((n_batch * q_len, hkv * gw), BF16),
        compiler_params=pltpu.CompilerParams(
            dimension_semantics=("parallel", "parallel", "parallel"),
            vmem_limit_bytes=_vmem_limit(tile_bytes)),
        name="gqa_attention",
    )(p_all, q_norm[:, None, :], cos, sin, *([k] * len(segs)), *([v] * len(segs)))


def _rope_angles(pos, dim):
    inv_freq = ROPE_THETA ** (-jnp.arange(0, dim, 2, dtype=F32) / dim)
    return pos.astype(F32)[:, None] * inv_freq[None, :]


def _rope_tables(angle_groups, n_pad_rows, reps):
    cos = jnp.concatenate([jnp.concatenate([jnp.cos(a), jnp.cos(a)], axis=1) for a in angle_groups], axis=1)
    sin = jnp.concatenate([jnp.concatenate([-jnp.sin(a), jnp.sin(a)], axis=1) for a in angle_groups], axis=1)
    w = cos.shape[1]
    cos = jnp.concatenate([jnp.tile(cos, (reps, 1)), jnp.ones((n_pad_rows, w), F32)], axis=0)
    sin = jnp.concatenate([jnp.tile(sin, (reps, 1)), jnp.zeros((n_pad_rows, w), F32)], axis=0)
    return cos, sin


def kernel(x, c, ctx, c_ctx, w_ada, b_ada, norm1_w, norm2_w, w_in, rwkv_mu_prev, rwkv_mu_next, rwkv_w0, rwkv_w2, rwkv_a0, rwkv_a2, rwkv_g2, rwkv_k_k, rwkv_k_a, rwkv_r_k, rwkv_ln_w, rwkv_ln_b, w_rwkv_out, ret_log_decay, w_ret_out, attn_q_norm, attn_k_norm, w_attn_out, w_out, w_ffn_up, ffn_conv_w, ffn_conv_b, w_ffn_down):
    B, S, D = x.shape
    NC = ctx.shape[1]
    NX = B * S
    R = NX + B * NC
    depth = w_in.shape[0]
    rw_dim = w_rwkv_out.shape[1]
    rw_heads = rw_dim // RWKV_HEAD
    lora_w = rwkv_w2.shape[2]
    lora_a = rwkv_a2.shape[2]
    rw_proj = rwkv_mu_prev.shape[1]
    ret_dim = w_ret_out.shape[1]
    ret_heads = ret_dim // RET_V_HEAD
    ret_qk = RET_V_HEAD // 2
    nqk = ret_heads * ret_qk
    ret_proj = 2 * nqk + 2 * ret_dim
    att_dim = w_attn_out.shape[1]
    att_heads = att_dim // ATT_HEAD
    att_kv = att_heads // ATT_GROUP
    att_proj = att_dim + 2 * att_kv * ATT_HEAD
    mix_proj = rw_proj + ret_proj + att_proj
    ffn_dim = w_ffn_down.shape[1]

    col_ret = rw_proj + (-rw_proj % RET_V_HEAD)
    col_att = col_ret + ret_proj
    col_att += -col_att % (ATT_GROUP * ATT_HEAD)
    col_gate = col_att + att_proj
    col_gate += -col_gate % 512
    n_proj = col_gate + (w_in.shape[-1] - mix_proj)
    tail_w = col_ret - 3 * rw_dim

    w_in_p = jnp.zeros((depth, D, n_proj), BF16)
    w_in_p = w_in_p.at[..., :rw_proj].set(w_in[..., :rw_proj].astype(BF16))
    w_in_p = w_in_p.at[..., col_ret:col_ret + ret_proj].set(w_in[..., rw_proj:rw_proj + ret_proj].astype(BF16))
    w_in_p = w_in_p.at[..., col_att:col_att + att_proj].set(w_in[..., rw_proj + ret_proj:mix_proj].astype(BF16))
    w_in_p = w_in_p.at[..., col_gate:].set(w_in[..., mix_proj:].astype(BF16))
    w_up_bf = w_ffn_up.astype(BF16)
    w_down_bf = w_ffn_down.astype(BF16)
    w_out_bf = w_out.astype(BF16)
    w_ro_bf, w_to_bf, w_ao_bf = w_rwkv_out.astype(BF16), w_ret_out.astype(BF16), w_attn_out.astype(BF16)
    conv_b3 = ffn_conv_b[:, None, :]

    lora_g = rwkv_g2.shape[1]
    mu_pad = ((0, 0), (0, col_ret - rw_proj))
    mu_prev_p = jnp.pad(rwkv_mu_prev, mu_pad)[:, None, :]
    mu_next_p = jnp.pad(rwkv_mu_next, mu_pad)[:, None, :]
    rw_vecs = jnp.stack([rwkv_k_k, rwkv_k_a, rwkv_r_k.reshape(depth, rw_dim)], axis=1)
    rw_bias = jnp.concatenate([rwkv_w0[:, 0], rwkv_a0[:, 0], rwkv_w0[:, 1], rwkv_a0[:, 1],
                               jnp.zeros((depth, rw_dim), F32)], axis=-1)[:, None, :]
    rw_lora = jnp.zeros((depth, tail_w, 5 * rw_dim), F32)
    for d in range(2):
        rw_lora = rw_lora.at[:, :lora_w, 2 * d * rw_dim:(2 * d + 1) * rw_dim].set(rwkv_w2[:, d])
        rw_lora = rw_lora.at[:, lora_w:lora_w + lora_a, (2 * d + 1) * rw_dim:(2 * d + 2) * rw_dim].set(rwkv_a2[:, d])
    rw_lora = rw_lora.at[:, lora_w + lora_a:lora_w + lora_a + lora_g, 4 * rw_dim:].set(rwkv_g2).astype(BF16)
    rw_ln = jnp.stack([rwkv_ln_w, rwkv_ln_b], axis=1)
    head_id = jnp.arange(rw_dim) // RWKV_HEAD
    ones_bd = (head_id[:, None] == head_id[None, :]).astype(BF16)

    pos = jnp.arange(S)
    att_cos, att_sin = _rope_tables([_rope_angles(pos // GRID_W, ATT_HEAD // 2),
                                     _rope_angles(pos % GRID_W, ATT_HEAD // 2)], B * NC, B)
    ret_cos, ret_sin = _rope_tables([_rope_angles(pos, ret_qk)], B * NC, B)

    silu_all = jnp.zeros((8, D), F32).at[:B].set(jax.nn.silu(c)).at[B].set(jax.nn.silu(c_ctx))

    h = jnp.concatenate([x.reshape(NX, D), ctx.reshape(B * NC, D)], axis=0)
    for l in range(depth):
        ctx_out = l < depth - 1
        n_rows = R if ctx_out else NX
        tm_rows = _pick(n_rows, (1088, 1024, 512, 256))
        mod = matmul(silu_all, w_ada, layer=l, tm=8, tn=1024) + b_ada[l][None, :]
        n1 = norm_mod(h, norm1_w, mod, 0, 1, layer=l, rows=R, n_batch=B, seq=S)
        p_all = matmul(n1, w_in_p, layer=l, tn=512)

        r_, v_, kk, lw, kd, a_sig, bonus, rw_gate = rwkv_prep(
            p_all, mu_prev_p, mu_next_p, rw_vecs, rw_bias, rw_lora, ones_bd, layer=l, width=col_ret,
            dim=rw_dim, lora_w=lora_w, lora_a=lora_a, n_x=NX, seq=S, n_ctx=NC)
        y2 = rwkv_scan(r_, v_, kk, lw, kd, a_sig, n_batch=B, seq=S, n_ctx=NC)
        rw_out = rwkv_finish(y2, bonus, rw_gate, rw_ln, ones_bd, layer=l)

        o2 = retention_scan(p_all, col_ret, ret_heads, ret_cos, ret_sin, ret_log_decay[l],
                            n_batch=B, seq=S, n_ctx=NC)
        rt_out = retention_finish(o2, p_all, col_ret + 2 * nqk + ret_dim)

        ka, va = attention_kv_prep(p_all, col_att + att_dim, att_kv * ATT_HEAD, attn_k_norm, att_cos, att_sin,
                                   layer=l)
        att_args = dict(layer=l, n_batch=B)
        at_out = gqa_attention(p_all, col_att, attn_q_norm, att_cos, att_sin, ka, va, q_row0=0, q_len=S,
                               segs=((NX, NC), (0, S)), **att_args)
        if ctx_out:
            at_c = gqa_attention(p_all, col_att, attn_q_norm, att_cos, att_sin, ka, va, q_row0=NX, q_len=NC,
                                 segs=((NX, NC),), **att_args)
            at_out = jnp.concatenate([at_out, at_c], axis=0)

        m = merge_branches(rw_out, rt_out, at_out, w_ro_bf, w_to_bf, w_ao_bf, p_all, col_gate,
                           layer=l, rows=n_rows, tm=tm_rows)
        h1 = matmul_residual(m, w_out_bf, h, mod, 2, layer=l, n_batch=B, seq=S, tm=tm_rows, tn=512)
        n2 = norm_mod(h1, norm2_w, mod, 3, 4, layer=l, rows=n_rows, n_batch=B, seq=S)
        act = ffn_up_conv_gate(n2, w_up_bf, ffn_conv_w, conv_b3, layer=l, n_x=NX, seq=S, n_ctx=NC, tm=tm_rows, tn=256)
        h = matmul_residual(act, w_down_bf, h1, mod, 5, layer=l, n_batch=B, seq=S, tm=tm_rows, tn=512,
                            tk=ffn_dim // 2)
    return h.reshape(B, S, D)
```

```python
import functools

import jax
import jax.numpy as jnp
from jax import lax
from jax.experimental import pallas as pl
from jax.experimental.pallas import tpu as pltpu

F32 = jnp.float32
BF16 = jnp.bfloat16

V7X_VMEM_LIMIT_CAP = 56 * 1024 * 1024
LANE = 128
BF16_SUBLANES = 16

GRID_W = 64
RWKV_HEAD = 64
RWKV_LN_EPS = 64e-5
RWKV_CHUNK = 64
RET_V_HEAD = 256
RET_CHUNK = 128
RET_NORM_EPS = 1e-6
ATT_HEAD = 128
ATT_GROUP = 4
ROPE_THETA = 10000.0
NORM_EPS = 1e-6
LOG2_E = 1.4426950408889634

_NT = (((1,), (1,)), ((), ()))
_TN = (((0,), (0,)), ((), ()))


def _vmem_limit(tile_bytes):
    return int(min(max(2 * tile_bytes + (8 << 20), 16 << 20), V7X_VMEM_LIMIT_CAP))


def _dot(x, y, dims=(((1,), (0,)), ((), ()))):
    return lax.dot_general(x.astype(BF16), y.astype(BF16), dims, preferred_element_type=F32)


def _split_bf16(x):
    hi = x.astype(BF16)
    return hi, (x - hi.astype(F32)).astype(BF16)


def _dot3(x, y):
    xh, xl = _split_bf16(x)
    yh, yl = _split_bf16(y)
    return jnp.dot(jnp.concatenate([xh, xh, xl], axis=1), jnp.concatenate([yh, yl, yh], axis=0),
                   preferred_element_type=F32)


def _mm_kernel(a_ref, b_ref, o_ref, *, nk):
    part = _dot(a_ref[...], b_ref[...])
    if nk == 1:
        o_ref[...] = part.astype(o_ref.dtype)
    else:
        k = pl.program_id(2)

        @pl.when(k == 0)
        def _():
            o_ref[...] = part

        @pl.when(k > 0)
        def _():
            o_ref[...] += part


def _pick(n, prefs):
    for p in prefs:
        if n % p == 0:
            return p
    return n


def matmul(a, b, *, layer=None, rows=None, tm=None, tn=None, tk=None, out_dtype=F32):
    M, K = a.shape
    M = rows or M
    N = b.shape[-1]
    assert b.shape[-2] == K
    tm = tm or _pick(M, (1088, 1024, 512, 256, 128, 8))
    tn = tn or _pick(N, (1024, 640, 512, 256, 128))
    tk = tk or K
    assert M % tm == 0 and N % tn == 0 and K % tk == 0, (M, N, K, tm, tn, tk)
    nk = K // tk
    if nk > 1:
        assert out_dtype == F32
    if b.ndim == 3:
        b_spec = pl.BlockSpec((None, tk, tn), lambda i, j, k: (layer, k, j))
    else:
        b_spec = pl.BlockSpec((tk, tn), lambda i, j, k: (k, j))
    tile_bytes = (tm * tk * a.dtype.itemsize + tk * tn * (b.dtype.itemsize + 2)
                  + tm * tn * (jnp.dtype(out_dtype).itemsize + 4))
    return pl.pallas_call(
        functools.partial(_mm_kernel, nk=nk),
        grid=(M // tm, N // tn, nk),
        in_specs=[pl.BlockSpec((tm, tk), lambda i, j, k: (i, k)), b_spec],
        out_specs=pl.BlockSpec((tm, tn), lambda i, j, k: (i, j)),
        out_shape=jax.ShapeDtypeStruct((M, N), out_dtype),
        compiler_params=pltpu.CompilerParams(
            dimension_semantics=("parallel", "parallel", "arbitrary"),
            vmem_limit_bytes=_vmem_limit(tile_bytes)),
        name="matmul",
    )(a, b)


def _row_ids(row0, tm):
    return row0 + lax.broadcasted_iota(jnp.int32, (tm, 1), 0)


def _seq_edges(g, n_x, seq, n_ctx):
    in_x = g < n_x
    starts = jnp.where(in_x, g & (seq - 1), (g - n_x) & (n_ctx - 1)) == 0
    ends = jnp.where(in_x, (g + 1) & (seq - 1), (g + 1 - n_x) & (n_ctx - 1)) == 0
    return starts, ends


def _row_select(table, g, n_batch, seq):
    out = table[n_batch:n_batch + 1]
    seq_id = g >> (seq.bit_length() - 1)
    for b in range(n_batch):
        out = jnp.where(seq_id == b, table[b:b + 1], out)
    return out


def _head_sum(x, ones_bd):
    hi, lo = _split_bf16(x)
    return (jnp.dot(hi, ones_bd, preferred_element_type=F32)
            + jnp.dot(lo, ones_bd, preferred_element_type=F32))


def _mm_res_kernel(a_ref, b_ref, h_ref, g_ref, o_ref, *, nk, tm, n_batch, seq):
    part = _dot(a_ref[...], b_ref[...])
    k = pl.program_id(2)
    row0 = pl.program_id(0) * tm

    def finish(acc):
        gate = _row_select(g_ref[...], _row_ids(row0, tm), n_batch, seq)
        o_ref[...] = h_ref[...] + gate * acc

    if nk == 1:
        finish(part)
    else:
        @pl.when(k == 0)
        def _():
            o_ref[...] = part

        @pl.when((k > 0) & (k < nk - 1))
        def _():
            o_ref[...] += part

        @pl.when(k == nk - 1)
        def _():
            finish(o_ref[...] + part)


def matmul_residual(a, b, h, mod, chunk, *, layer, n_batch, seq, tm, tn, tk=None):
    M, K = a.shape
    N = b.shape[-1]
    tk = tk or K
    assert M % tm == 0 and N % tn == 0 and K % tk == 0
    nk = K // tk
    nn = N // tn
    tile_bytes = tm * tk * 2 + tk * tn * 2 + 3 * tm * tn * 4
    return pl.pallas_call(
        functools.partial(_mm_res_kernel, nk=nk, tm=tm, n_batch=n_batch, seq=seq),
        grid=(M // tm, nn, nk),
        in_specs=[pl.BlockSpec((tm, tk), lambda i, j, k: (i, k)),
                  pl.BlockSpec((None, tk, tn), lambda i, j, k: (layer, k, j)),
                  pl.BlockSpec((tm, tn), lambda i, j, k: (i, j)),
                  pl.BlockSpec((8, tn), lambda i, j, k: (0, chunk * nn + j))],
        out_specs=pl.BlockSpec((tm, tn), lambda i, j, k: (i, j)),
        out_shape=jax.ShapeDtypeStruct((M, N), F32),
        compiler_params=pltpu.CompilerParams(
            dimension_semantics=("parallel", "parallel", "arbitrary"),
            vmem_limit_bytes=_vmem_limit(tile_bytes)),
        name="matmul_residual",
    )(a, b, h, mod)


def _norm_mod_kernel(h_ref, w_ref, sc_ref, sh_ref, o_ref, *, tm, n_batch, seq):
    x = h_ref[...]
    g = _row_ids(pl.program_id(0) * tm, tm)
    y = x * lax.rsqrt(jnp.mean(x * x, axis=-1, keepdims=True) + NORM_EPS) * w_ref[...]
    y = y * (1.0 + _row_select(sc_ref[...], g, n_batch, seq)) + _row_select(sh_ref[...], g, n_batch, seq)
    o_ref[...] = y.astype(o_ref.dtype)


def norm_mod(h, norm_w, mod, shift_chunk, scale_chunk, *, layer, rows, n_batch, seq, tm=256):
    D = h.shape[1]
    assert rows % tm == 0
    return pl.pallas_call(
        functools.partial(_norm_mod_kernel, tm=tm, n_batch=n_batch, seq=seq),
        grid=(rows // tm,),
        in_specs=[pl.BlockSpec((tm, D), lambda i: (i, 0)),
                  pl.BlockSpec((None, 1, D), lambda i: (layer, 0, 0)),
                  pl.BlockSpec((8, D), lambda i: (0, scale_chunk)),
                  pl.BlockSpec((8, D), lambda i: (0, shift_chunk))],
        out_specs=pl.BlockSpec((tm, D), lambda i: (i, 0)),
        out_shape=jax.ShapeDtypeStruct((rows, D), BF16),
        compiler_params=pltpu.CompilerParams(
            dimension_semantics=("parallel",), vmem_limit_bytes=_vmem_limit(tm * D * 10)),
        name="norm_mod",
    )(h, norm_w[:, None, :], mod, mod)


def _ffn_up_kernel(a_ref, ap_ref, an_ref, wv_ref, wg_ref, cv_ref, cg_ref, bv_ref, bg_ref, o_ref, ext_ref,
                   *, tm, halo, n_x, seq, n_ctx):
    i = pl.program_id(0)
    j = pl.program_id(1)

    @pl.when(j == 0)
    def _():
        ext_ref[0:halo, :] = ap_ref[...]
        ext_ref[halo:halo + tm, :] = a_ref[...]
        ext_ref[halo + tm:, :] = an_ref[...]

    starts, ends = _seq_edges(_row_ids(i * tm, tm), n_x, seq, n_ctx)
    a_ext = ext_ref[...]

    def conv(w_ref, c_ref, b_ref):
        u = jnp.dot(a_ext, w_ref[...], preferred_element_type=F32)
        prev = jnp.where(starts, 0.0, u[halo - 1:halo - 1 + tm])
        nxt = jnp.where(ends, 0.0, u[halo + 1:halo + 1 + tm])
        c = c_ref[...]
        return prev * c[0:1] + u[halo:halo + tm] * c[1:2] + nxt * c[2:3] + b_ref[...]

    val = conv(wv_ref, cv_ref, bv_ref)
    gate = conv(wg_ref, cg_ref, bg_ref)
    o_ref[...] = (jax.nn.silu(gate) * val).astype(o_ref.dtype)


def ffn_up_conv_gate(a, w_up, conv_w, conv_b, *, layer, n_x, seq, n_ctx, tm, tn=512):
    R, D = a.shape
    F = w_up.shape[-1] // 2
    halo = BF16_SUBLANES
    assert R % tm == 0 and tm % halo == 0 and F % tn == 0
    assert seq & (seq - 1) == 0 and n_ctx & (n_ctx - 1) == 0
    nf = F // tn
    nh = tm // halo
    last = R // halo - 1
    tile_bytes = (tm * D * 2 + 2 * D * tn * 2 + tm * tn * 2) + ((tm + 2 * halo) * D * 2) // 2 + 4 * tm * tn * 4
    kern = functools.partial(_ffn_up_kernel, tm=tm, halo=halo, n_x=n_x, seq=seq, n_ctx=n_ctx)
    return pl.pallas_call(
        kern,
        grid=(R // tm, nf),
        in_specs=[pl.BlockSpec((tm, D), lambda i, j: (i, 0)),
                  pl.BlockSpec((halo, D), lambda i, j: (jnp.maximum(i * nh - 1, 0), 0)),
                  pl.BlockSpec((halo, D), lambda i, j: (jnp.minimum((i + 1) * nh, last), 0)),
                  pl.BlockSpec((None, D, tn), lambda i, j: (layer, 0, j)),
                  pl.BlockSpec((None, D, tn), lambda i, j: (layer, 0, nf + j)),
                  pl.BlockSpec((None, 3, tn), lambda i, j: (layer, 0, j)),
                  pl.BlockSpec((None, 3, tn), lambda i, j: (layer, 0, nf + j)),
                  pl.BlockSpec((None, 1, tn), lambda i, j: (layer, 0, j)),
                  pl.BlockSpec((None, 1, tn), lambda i, j: (layer, 0, nf + j))],
        out_specs=pl.BlockSpec((tm, tn), lambda i, j: (i, j)),
        out_shape=jax.ShapeDtypeStruct((R, F), BF16),
        scratch_shapes=[pltpu.VMEM((tm + 2 * halo, D), BF16)],
        compiler_params=pltpu.CompilerParams(
            dimension_semantics=("parallel", "arbitrary"),
            vmem_limit_bytes=_vmem_limit(tile_bytes)),
        name="ffn_up_conv_gate",
    )(a, a, a, w_up, w_up, conv_w, conv_w, conv_b, conv_b)


def _merge_kernel(rw_ref, rt_ref, at_ref, w1_ref, w2_ref, w3_ref, g1_ref, g2_ref, g3_ref, o_ref):
    def br(x_ref, w_ref, g_ref):
        return jax.nn.sigmoid(g_ref[...]) * _dot(x_ref[...], w_ref[...])

    m = br(rw_ref, w1_ref, g1_ref) + br(rt_ref, w2_ref, g2_ref) + br(at_ref, w3_ref, g3_ref)
    o_ref[...] = m.astype(o_ref.dtype)


def merge_branches(rw, rt, at, w1, w2, w3, p_all, gate_col0, *, layer, rows, tm, tn=512):
    D = w1.shape[-1]
    assert rows % tm == 0 and D % tn == 0 and gate_col0 % tn == 0
    g0 = gate_col0 // tn
    nd = D // tn
    k1, k2, k3 = rw.shape[1], rt.shape[1], at.shape[1]
    ks = k1 + k2 + k3
    tile_bytes = tm * ks * 2 + ks * tn * 2 + 3 * tm * tn * 4 + tm * tn * 2 + 3 * tm * tn * 4

    def gspec(i):
        return pl.BlockSpec((tm, tn), lambda m, n, i=i: (m, g0 + i * nd + n))

    def wspec(k):
        return pl.BlockSpec((None, k, tn), lambda m, n: (layer, 0, n))

    return pl.pallas_call(
        _merge_kernel,
        grid=(rows // tm, nd),
        in_specs=[pl.BlockSpec((tm, k1), lambda m, n: (m, 0)),
                  pl.BlockSpec((tm, k2), lambda m, n: (m, 0)),
                  pl.BlockSpec((tm, k3), lambda m, n: (m, 0)),
                  wspec(k1), wspec(k2), wspec(k3),
                  gspec(0), gspec(1), gspec(2)],
        out_specs=pl.BlockSpec((tm, tn), lambda m, n: (m, n)),
        out_shape=jax.ShapeDtypeStruct((rows, D), BF16),
        compiler_params=pltpu.CompilerParams(
            dimension_semantics=("parallel", "parallel"),
            vmem_limit_bytes=_vmem_limit(tile_bytes)),
        name="merge_branches",
    )(rw, rt, at, w1, w2, w3, p_all, p_all, p_all)


def _softplus(z):
    return jnp.maximum(z, 0.0) + jnp.log(1.0 + jnp.exp(-jnp.abs(z)))


def _rwkv_prep_kernel(p_ref, pp_ref, pn_ref, mup_ref, mun_ref, vec_ref, bias_ref, lora_ref, ones_ref,
                      r_ref, v_ref, kk_ref, lw_ref, kd_ref, as_ref, bonus_ref, gate_ref,
                      *, tm, dim, lora_w, lora_a, n_x, seq, n_ctx):
    halo = pp_ref.shape[0]
    g = _row_ids(pl.program_id(0) * tm, tm)
    starts, ends = _seq_edges(g, n_x, seq, n_ctx)
    p = p_ref[...]
    ext = jnp.concatenate([pp_ref[...], p, pn_ref[...]], axis=0)
    prev = jnp.where(starts, 0.0, ext[halo - 1:halo - 1 + tm])
    nxt = jnp.where(ends, 0.0, ext[halo + 1:halo + 1 + tm])
    x = p + mup_ref[...] * (prev - p) + mun_ref[...] * (nxt - p)

    r = x[:, :dim]
    k = x[:, dim:2 * dim]
    v = x[:, 2 * dim:3 * dim]
    tail = x[:, 3 * dim:]
    lane = lax.broadcasted_iota(jnp.int32, tail.shape, 1)
    act = jnp.where(lane < lora_w, jnp.tanh(tail), jnp.where(lane < lora_w + lora_a, tail, jax.nn.sigmoid(tail)))
    pre = _dot(act, lora_ref[...]) + bias_ref[...]

    vec = vec_ref[...]
    ones_bd = ones_ref[...]
    kk = k * vec[0:1]
    kk = kk * lax.rsqrt(jnp.maximum(_head_sum(kk * kk, ones_bd), 1e-24))
    r_ref[...] = r
    v_ref[...] = v
    kk_ref[...] = kk
    bonus = jnp.zeros_like(r)
    for d in range(2):
        w_pre = pre[:, 2 * d * dim:(2 * d + 1) * dim]
        a_pre = pre[:, (2 * d + 1) * dim:(2 * d + 2) * dim]
        lw_ref[d] = -jnp.exp(-_softplus(-w_pre) - 0.5)
        a_sig = jax.nn.sigmoid(a_pre)
        kd = k * (1.0 + (a_sig - 1.0) * vec[1:2])
        as_ref[d] = a_sig
        kd_ref[d] = kd
        bonus = bonus + _head_sum(r * kd * vec[2:3], ones_bd)
    bonus_ref[...] = bonus * v
    gate_ref[...] = pre[:, 4 * dim:]


def rwkv_prep(p_all, mu_prev, mu_next, vecs, bias, lora, ones_bd, *, layer, width, dim, lora_w, lora_a,
              n_x, seq, n_ctx, tm=128):
    R = p_all.shape[0]
    halo = 8
    assert R % tm == 0 and tm % halo == 0
    nh = tm // halo
    last = R // halo - 1
    row = pl.BlockSpec((tm, dim), lambda i: (i, 0))
    per_dir = pl.BlockSpec((2, tm, dim), lambda i: (0, i, 0))

    def par(a):
        return pl.BlockSpec((None,) + a.shape[1:], lambda i: (layer,) + (0,) * (a.ndim - 1))

    f = jax.ShapeDtypeStruct((R, dim), F32)
    f2 = jax.ShapeDtypeStruct((2, R, dim), F32)
    kern = functools.partial(_rwkv_prep_kernel, tm=tm, dim=dim, lora_w=lora_w, lora_a=lora_a,
                             n_x=n_x, seq=seq, n_ctx=n_ctx)
    return pl.pallas_call(
        kern,
        grid=(R // tm,),
        in_specs=[pl.BlockSpec((tm, width), lambda i: (i, 0)),
                  pl.BlockSpec((halo, width), lambda i: (jnp.maximum(i * nh - 1, 0), 0)),
                  pl.BlockSpec((halo, width), lambda i: (jnp.minimum((i + 1) * nh, last), 0)),
                  par(mu_prev), par(mu_next), par(vecs), par(bias), par(lora),
                  pl.BlockSpec(ones_bd.shape, lambda i: (0, 0))],
        out_specs=[row, row, row, per_dir, per_dir, per_dir, row, row],
        out_shape=[f, f, f, f2, f2, f2, f, f],
        compiler_params=pltpu.CompilerParams(
            dimension_semantics=("parallel",),
            vmem_limit_bytes=_vmem_limit(tm * (width + 20 * dim) * 4 + lora.shape[1] * lora.shape[2] * 2
                                         + dim * dim * 2)),
        name="rwkv_prep",
    )(p_all, p_all, p_all, mu_prev, mu_next, vecs, bias, lora, ones_bd)


def _rwkv_finish_kernel(y_ref, bonus_ref, gate_ref, ln_ref, ones_ref, o_ref, *, head):
    ones_bd = ones_ref[...]
    y = y_ref[0] + y_ref[1]
    yc = y - _head_sum(y, ones_bd) * (1.0 / head)
    var = _head_sum(yc * yc, ones_bd) * (1.0 / head)
    ln = ln_ref[...]
    out = yc * lax.rsqrt(var + RWKV_LN_EPS) * ln[0:1] + ln[1:2] + bonus_ref[...]
    o_ref[...] = (out * gate_ref[...]).astype(o_ref.dtype)


def rwkv_finish(y2, bonus, gate, ln_wb, ones_bd, *, layer, tm=256):
    _, R, dim = y2.shape
    assert R % tm == 0
    row = pl.BlockSpec((tm, dim), lambda i: (i, 0))
    return pl.pallas_call(
        functools.partial(_rwkv_finish_kernel, head=RWKV_HEAD),
        grid=(R // tm,),
        in_specs=[pl.BlockSpec((2, tm, dim), lambda i: (0, i, 0)), row, row,
                  pl.BlockSpec((None, 2, dim), lambda i: (layer, 0, 0)),
                  pl.BlockSpec(ones_bd.shape, lambda i: (0, 0))],
        out_specs=row,
        out_shape=jax.ShapeDtypeStruct((R, dim), BF16),
        compiler_params=pltpu.CompilerParams(
            dimension_semantics=("parallel",), vmem_limit_bytes=_vmem_limit(tm * dim * 40 + dim * dim * 2)),
        name="rwkv_finish",
    )(y2, bonus, gate, ln_wb, ones_bd)


def _row_block_fn(n_batch, seq, n_ctx, chunk):
    nctx = n_ctx // chunk
    nx = seq // chunk
    coff = n_batch * nx

    def rb(b, d, c):
        fwd = jnp.where(c < nctx, coff + b * nctx + c, b * nx + (c - nctx))
        bwd = jnp.where(c < nctx, coff + b * nctx + (nctx - 1 - c), b * nx + (nctx + nx - 1 - c))
        return jnp.where(d == 0, fwd, bwd)

    return rb, nctx + nx


def _rwkv_kernel(r_ref, v_ref, kk_ref, lw_ref, kd_ref, as_ref, y_ref, s_ref, *, C, npair):
    fwd = pl.program_id(1) == 0

    @pl.when(pl.program_id(2) == 0)
    def _():
        s_ref[...] = jnp.zeros_like(s_ref)

    W = 2 * C
    row = lax.broadcasted_iota(jnp.int32, (C, W), 0)
    lane = lax.broadcasted_iota(jnp.int32, (C, W), 1)
    col = lane & (C - 1)
    ahead = jnp.where(fwd, row - col, col - row)
    strict = ahead > 0
    lower = ahead >= 0
    left = lane < C
    eye = jnp.where(ahead == 0, 1.0, 0.0)
    r0 = lax.broadcasted_iota(jnp.int32, (C, C), 0)
    c0 = lax.broadcasted_iota(jnp.int32, (C, C), 1)
    tri = jnp.where(jnp.where(fwd, r0 - c0, c0 - r0) >= 0, 1.0, 0.0).astype(BF16)
    bd_mask = (lax.broadcasted_iota(jnp.int32, (W, W), 0) < C) == (lax.broadcasted_iota(jnp.int32, (W, W), 1) < C)

    def only(x, first):
        return jnp.where(left, x, 0.0) if first else jnp.where(left, 0.0, x)

    def stack(*xs):
        return jnp.concatenate(xs, axis=0)

    def bdiag(p):
        return stack(only(p, True), only(p, False))

    P = range(npair)
    sls = [slice(p * W, (p + 1) * W) for p in P]
    r = [r_ref[:, sl] for sl in sls]
    v = [v_ref[:, sl] for sl in sls]
    kk = [kk_ref[:, sl] for sl in sls]
    lw = [lw_ref[:, sl] for sl in sls]
    kd = [kd_ref[:, sl] for sl in sls]
    b = [kk[p] * as_ref[:, sls[p]] for p in P]
    s = [s_ref[p] for p in P]

    def cumsum(x):
        l1 = x.astype(BF16)
        e1 = x - l1.astype(F32)
        l2 = e1.astype(BF16)
        l3 = (e1 - l2.astype(F32)).astype(BF16)
        g3 = _dot(tri, jnp.concatenate([l1, l2, l3], axis=1))
        return g3[:, :W] + g3[:, W:2 * W] + g3[:, 2 * W:]

    g_in = [cumsum(lw[p]) for p in P]
    g_last = [jnp.where(fwd, g[C - 1:C, :], g[0:1, :]) for g in g_in]
    e_neg = [jnp.exp(-g) for g in g_in]
    a_t = [-kk[p] * jnp.exp(g_in[p] - lw[p]) for p in P]
    r_t = [r[p] * jnp.exp(g_in[p]) for p in P]
    b_t = [b[p] * e_neg[p] for p in P]
    k_t = [kd[p] * e_neg[p] for p in P]
    e_rem = [jnp.exp(g_last[p] - g_in[p]) for p in P]
    b_h = [b[p] * e_rem[p] for p in P]
    k_h = [kd[p] * e_rem[p] for p in P]

    res_a = [_dot(stack(only(a_t[p], True), only(r_t[p], True)), stack(k_t[p], b_t[p]), _NT) for p in P]
    res_b = [_dot(stack(only(a_t[p], False), only(r_t[p], False)), stack(b_t[p], k_t[p]), _NT) for p in P]
    ak = [jnp.where(strict, jnp.where(left, res_a[p][:C], res_b[p][:C]), 0.0) for p in P]
    ab = [jnp.where(strict, jnp.where(left, res_b[p][:C], res_a[p][:C]), 0.0) for p in P]
    rk = [jnp.where(lower, jnp.where(left, res_a[p][C:], res_b[p][C:]), 0.0) for p in P]
    rb = [jnp.where(lower, jnp.where(left, res_b[p][C:], res_a[p][C:]), 0.0) for p in P]

    inv = [eye + ab[p] for p in P]
    pw = [_dot3(ab[p], bdiag(ab[p])) for p in P]
    n = 2
    while n < C:
        both = [_dot3(stack(inv[p], pw[p]), bdiag(pw[p])) for p in P]
        inv = [inv[p] + both[p][:C] for p in P]
        pw = [both[p][C:] for p in P]
        n *= 2

    a_s = [_dot(stack(a_t[p], r_t[p]), s[p], _NT) for p in P]
    inner = [a_s[p][:C] + _dot(ak[p], stack(only(v[p], True), only(v[p], False))) for p in P]
    u = [_dot(inv[p], stack(only(inner[p], False), only(inner[p], True))) for p in P]
    for p in P:
        y_ref[:, sls[p]] = a_s[p][C:] + _dot(
            jnp.concatenate([rb[p], rk[p]], axis=1),
            stack(only(u[p], False), only(u[p], True), only(v[p], True), only(v[p], False)))
    for p in P:
        upd = _dot(stack(u[p], v[p]), stack(b_h[p], k_h[p]), _TN)
        s_ref[p] = s[p] * jnp.exp(g_last[p]) + jnp.where(bd_mask, upd, 0.0)


def rwkv_scan(r, v, kk, lw, kd, a_sig, *, n_batch, seq, n_ctx, C=RWKV_CHUNK):
    R, HN = r.shape
    npair = HN // (2 * RWKV_HEAD)
    rb, steps = _row_block_fn(n_batch, seq, n_ctx, C)
    shared = pl.BlockSpec((C, HN), lambda b, d, c: (rb(b, d, c), 0))
    per_dir = pl.BlockSpec((None, C, HN), lambda b, d, c: (d, rb(b, d, c), 0))
    return pl.pallas_call(
        functools.partial(_rwkv_kernel, C=C, npair=npair),
        grid=(n_batch, 2, steps),
        in_specs=[shared, shared, shared, per_dir, per_dir, per_dir],
        out_specs=per_dir,
        out_shape=jax.ShapeDtypeStruct((2, R, HN), F32),
        scratch_shapes=[pltpu.VMEM((npair, 2 * RWKV_HEAD, 2 * RWKV_HEAD), F32)],
        compiler_params=pltpu.CompilerParams(dimension_semantics=("parallel", "parallel", "arbitrary")),
        name="rwkv_scan",
    )(r, v, kk, lw, kd, a_sig)


def _ret_kernel(lg_ref, q_ref, k_ref, v_ref, cos_ref, sin_ref, o_ref, s_ref, *, C, H, dk, dv):
    fwd = pl.program_id(1) == 0

    @pl.when(pl.program_id(2) == 0)
    def _():
        s_ref[...] = jnp.zeros_like(s_ref)

    cos = cos_ref[...]
    sin = sin_ref[...]
    row = lax.broadcasted_iota(jnp.int32, (C, C), 0)
    col = lax.broadcasted_iota(jnp.int32, (C, C), 1)
    ahead = jnp.where(fwd, row - col, col - row)
    dist = jnp.maximum(ahead, 0).astype(F32)
    idx = lax.broadcasted_iota(jnp.int32, (C, 1), 0)
    pos = jnp.where(fwd, idx, C - 1 - idx).astype(F32)

    hs = range(H)
    lg = [lg_ref[h][:, 0:1] for h in hs]
    q = [q_ref[:, h * dk:(h + 1) * dk] for h in hs]
    k = [k_ref[:, h * dk:(h + 1) * dk] * (dk ** -0.5) for h in hs]
    q = [x * cos + pltpu.roll(x, dk // 2, 1) * sin for x in q]
    k = [x * cos + pltpu.roll(x, dk // 2, 1) * sin for x in k]
    v = [v_ref[:, h * dv:(h + 1) * dv] for h in hs]
    s = [s_ref[h] for h in hs]
    scores = [_dot(q[h], k[h], _NT) * jnp.where(ahead >= 0, jnp.exp(dist * lg[h]), 0.0) for h in hs]
    carry = [_dot(q[h] * jnp.exp((pos + 1.0) * lg[h]), s[h]) for h in hs]
    for h in hs:
        o_ref[:, h * dv:(h + 1) * dv] = _dot(scores[h], v[h]) + carry[h]
    for h in hs:
        s_ref[h] = s[h] * jnp.exp(C * lg[h]) + _dot(k[h] * jnp.exp((C - 1.0 - pos) * lg[h]), v[h], _TN)


def retention_scan(p_ret, H, cos2, sin2, lg, *, n_batch, seq, n_ctx, C=RET_CHUNK):
    R = p_ret.shape[0]
    dv = RET_V_HEAD
    dk = dv // 2
    rb, steps = _row_block_fn(n_batch, seq, n_ctx, C)
    lg_b = jnp.broadcast_to(lg.astype(F32).reshape(2, H, 1, 1), (2, H, 1, LANE))
    tab = pl.BlockSpec((C, dk), lambda b, d, c: (rb(b, d, c), 0))
    return pl.pallas_call(
        functools.partial(_ret_kernel, C=C, H=H, dk=dk, dv=dv),
        grid=(n_batch, 2, steps),
        in_specs=[pl.BlockSpec((None, H, 1, LANE), lambda b, d, c: (d, 0, 0, 0)),
                  pl.BlockSpec((C, H * dk), lambda b, d, c: (rb(b, d, c), 0)),
                  pl.BlockSpec((C, H * dk), lambda b, d, c: (rb(b, d, c), 1)),
                  pl.BlockSpec((C, H * dv), lambda b, d, c: (rb(b, d, c), 1)),
                  tab, tab],
        out_specs=pl.BlockSpec((None, C, H * dv), lambda b, d, c: (d, rb(b, d, c), 0)),
        out_shape=jax.ShapeDtypeStruct((2, R, H * dv), F32),
        scratch_shapes=[pltpu.VMEM((H, dk, dv), F32)],
        compiler_params=pltpu.CompilerParams(dimension_semantics=("parallel", "parallel", "arbitrary")),
        name="retention_scan",
    )(lg_b, p_ret, p_ret, p_ret, cos2, sin2)


def _ret_finish_kernel(o_ref, g_ref, out_ref):
    o = o_ref[0] + o_ref[1]
    oc = o - jnp.mean(o, axis=-1, keepdims=True)
    y = oc * lax.rsqrt(jnp.mean(oc * oc, axis=-1, keepdims=True) + RET_NORM_EPS)
    out_ref[...] = (y * jax.nn.silu(g_ref[...])).astype(out_ref.dtype)


def retention_finish(o2, p_all, gcol0, *, tm=512):
    _, R, HD = o2.shape
    dv = RET_V_HEAD
    assert R % tm == 0 and gcol0 % dv == 0
    g0 = gcol0 // dv
    return pl.pallas_call(
        _ret_finish_kernel,
        grid=(R // tm, HD // dv),
        in_specs=[pl.BlockSpec((2, tm, dv), lambda i, h: (0, i, h)),
                  pl.BlockSpec((tm, dv), lambda i, h: (i, g0 + h))],
        out_specs=pl.BlockSpec((tm, dv), lambda i, h: (i, h)),
        out_shape=jax.ShapeDtypeStruct((R, HD), BF16),
        compiler_params=pltpu.CompilerParams(dimension_semantics=("parallel", "parallel")),
        name="retention_finish",
    )(o2, p_all)


def _norm_rope(x, w, cos, sin):
    quarter = x.shape[1] // 4
    y = x * lax.rsqrt(jnp.mean(x * x, axis=-1, keepdims=True) + NORM_EPS) * w
    lane = lax.broadcasted_iota(jnp.int32, y.shape, 1)
    first = (lane & (2 * quarter - 1)) < quarter
    swapped = jnp.where(first, pltpu.roll(y, 3 * quarter, 1), pltpu.roll(y, quarter, 1))
    return y * cos + swapped * sin


def _kv_prep_kernel(k_ref, v_ref, w_ref, cos_ref, sin_ref, ko_ref, vo_ref, *, dh):
    cos = cos_ref[...]
    sin = sin_ref[...]
    w = w_ref[...]
    for h in range(k_ref.shape[1] // dh):
        sl = slice(h * dh, (h + 1) * dh)
        ko_ref[:, sl] = _norm_rope(k_ref[:, sl], w, cos, sin).astype(ko_ref.dtype)
        vo_ref[sl, :] = v_ref[:, sl].T.astype(vo_ref.dtype)


def attention_kv_prep(p_att, kcol0, width, k_norm, cos, sin, *, layer, tm=512, dh=ATT_HEAD):
    R = p_att.shape[0]
    assert R % tm == 0 and kcol0 % width == 0
    kb = kcol0 // width
    tab = pl.BlockSpec((tm, dh), lambda i: (i, 0))
    return pl.pallas_call(
        functools.partial(_kv_prep_kernel, dh=dh),
        grid=(R // tm,),
        in_specs=[pl.BlockSpec((tm, width), lambda i: (i, kb)),
                  pl.BlockSpec((tm, width), lambda i: (i, kb + 1)),
                  pl.BlockSpec((None, 1, dh), lambda i: (layer, 0, 0)), tab, tab],
        out_specs=[pl.BlockSpec((tm, width), lambda i: (i, 0)),
                   pl.BlockSpec((width, tm), lambda i: (0, i))],
        out_shape=[jax.ShapeDtypeStruct((R, width), BF16), jax.ShapeDtypeStruct((width, R), BF16)],
        compiler_params=pltpu.CompilerParams(dimension_semantics=("parallel",)),
        name="attention_kv_prep",
    )(p_att, p_att, k_norm[:, None, :], cos, sin)


def _attn_kernel(q_ref, w_ref, cos_ref, sin_ref, *refs, nseg, group, dh):
    k_refs, vt_refs, o_ref = refs[:nseg], refs[nseg:2 * nseg], refs[2 * nseg]
    ks = [r[...] for r in k_refs]
    vts = [r[...] for r in vt_refs]
    cos = cos_ref[...]
    sin = sin_ref[...]
    w = w_ref[...] * (dh ** -0.5 * LOG2_E)

    def scores(g):
        q = _norm_rope(q_ref[:, g * dh:(g + 1) * dh], w, cos, sin).astype(BF16)
        return [lax.dot_general(k, q, _NT, preferred_element_type=F32) for k in ks]

    def softmax(ss):
        m = functools.reduce(jnp.maximum, [jnp.max(s, axis=0, keepdims=True) for s in ss])
        ps = [jnp.exp2(s - m) for s in ss]
        l = functools.reduce(jnp.add, [jnp.sum(p, axis=0, keepdims=True) for p in ps])
        return [p.astype(BF16) for p in ps], l

    def output(g, ps, l):
        ot = functools.reduce(jnp.add, [jnp.dot(vt, p, preferred_element_type=F32)
                                        for p, vt in zip(ps, vts)])
        o_ref[:, g * dh:(g + 1) * dh] = (ot / l).T.astype(o_ref.dtype)

    ss, pl_ = {}, {}
    for step in range(group + 2):
        if step < group:
            ss[step] = scores(step)
        if 0 <= step - 1 < group:
            pl_[step - 1] = softmax(ss.pop(step - 1))
        if 0 <= step - 2 < group:
            output(step - 2, *pl_.pop(step - 2))


def gqa_attention(p_att, q_norm, cos, sin, k, vt, *, layer, n_batch, q_row0, q_len, segs, tq=256,
                  group=ATT_GROUP, dh=ATT_HEAD):
    hkv = k.shape[1] // dh
    gw = group * dh
    assert q_len % tq == 0 and q_row0 % tq == 0
    nq = q_len // tq
    q0 = q_row0 // tq
    for row0, ln in segs:
        assert row0 % ln == 0
    k_specs = [pl.BlockSpec((ln, dh), lambda b, h, i, row0=row0, ln=ln: (row0 // ln + b, h))
               for row0, ln in segs]
    vt_specs = [pl.BlockSpec((dh, ln), lambda b, h, i, row0=row0, ln=ln: (h, row0 // ln + b))
                for row0, ln in segs]
    tk = sum(ln for _, ln in segs)
    tile_bytes = tq * gw * 6 + 2 * tk * dh * 2 + 3 * tq * tk * 4
    tab = pl.BlockSpec((tq, dh), lambda b, h, i: (q0 + b * nq + i, 0))
    return pl.pallas_call(
        functools.partial(_attn_kernel, nseg=len(segs), group=group, dh=dh),
        grid=(n_batch, hkv, nq),
        in_specs=[pl.BlockSpec((tq, gw), lambda b, h, i: (q0 + b * nq + i, h)),
                  pl.BlockSpec((None, 1, dh), lambda b, h, i: (layer, 0, 0)), tab, tab] + k_specs + vt_specs,
        out_specs=pl.BlockSpec((tq, gw), lambda b, h, i: (b * nq + i, h)),
        out_shape=jax.ShapeDtypeStruct((n_batch * q_len, hkv * gw), BF16),
        compiler_params=pltpu.CompilerParams(
            dimension_semantics=("parallel", "parallel", "parallel"),
            vmem_limit_bytes=_vmem_limit(tile_bytes)),
        name="gqa_attention",
    )(p_att, q_norm[:, None, :], cos, sin, *([k] * len(segs)), *([vt] * len(segs)))


def _rope_angles(pos, dim):
    inv_freq = ROPE_THETA ** (-jnp.arange(0, dim, 2, dtype=F32) / dim)
    return pos.astype(F32)[:, None] * inv_freq[None, :]


def _rope_tables(angle_groups, n_pad_rows, reps):
    cos = jnp.concatenate([jnp.concatenate([jnp.cos(a), jnp.cos(a)], axis=1) for a in angle_groups], axis=1)
    sin = jnp.concatenate([jnp.concatenate([-jnp.sin(a), jnp.sin(a)], axis=1) for a in angle_groups], axis=1)
    w = cos.shape[1]
    cos = jnp.concatenate([jnp.tile(cos, (reps, 1)), jnp.ones((n_pad_rows, w), F32)], axis=0)
    sin = jnp.concatenate([jnp.tile(sin, (reps, 1)), jnp.zeros((n_pad_rows, w), F32)], axis=0)
    return cos, sin


def kernel(x, c, ctx, c_ctx, w_ada, b_ada, norm1_w, norm2_w, w_in, rwkv_mu_prev, rwkv_mu_next, rwkv_w0, rwkv_w2, rwkv_a0, rwkv_a2, rwkv_g2, rwkv_k_k, rwkv_k_a, rwkv_r_k, rwkv_ln_w, rwkv_ln_b, w_rwkv_out, ret_log_decay, w_ret_out, attn_q_norm, attn_k_norm, w_attn_out, w_out, w_ffn_up, ffn_conv_w, ffn_conv_b, w_ffn_down):
    B, S, D = x.shape
    NC = ctx.shape[1]
    NX = B * S
    R = NX + B * NC
    depth = w_in.shape[0]
    rw_dim = w_rwkv_out.shape[1]
    rw_heads = rw_dim // RWKV_HEAD
    lora_w = rwkv_w2.shape[2]
    lora_a = rwkv_a2.shape[2]
    rw_proj = rwkv_mu_prev.shape[1]
    ret_dim = w_ret_out.shape[1]
    ret_heads = ret_dim // RET_V_HEAD
    ret_qk = RET_V_HEAD // 2
    nqk = ret_heads * ret_qk
    ret_proj = 2 * nqk + 2 * ret_dim
    att_dim = w_attn_out.shape[1]
    att_heads = att_dim // ATT_HEAD
    att_kv = att_heads // ATT_GROUP
    att_proj = att_dim + 2 * att_kv * ATT_HEAD
    mix_proj = rw_proj + ret_proj + att_proj
    ffn_dim = w_ffn_down.shape[1]

    rw_width = rw_proj + (-rw_proj % 512)
    tail_w = rw_width - 3 * rw_dim
    w_in_rw = jnp.pad(w_in[..., :rw_proj].astype(BF16), ((0, 0), (0, 0), (0, rw_width - rw_proj)))
    w_in_ret = w_in[..., rw_proj:rw_proj + ret_proj].astype(BF16)
    w_in_att = w_in[..., rw_proj + ret_proj:mix_proj].astype(BF16)
    w_in_gate = w_in[..., mix_proj:].astype(BF16)
    w_up_bf = w_ffn_up.astype(BF16)
    w_down_bf = w_ffn_down.astype(BF16)
    w_out_bf = w_out.astype(BF16)
    w_ro_bf, w_to_bf, w_ao_bf = w_rwkv_out.astype(BF16), w_ret_out.astype(BF16), w_attn_out.astype(BF16)
    conv_b3 = ffn_conv_b[:, None, :]

    lora_g = rwkv_g2.shape[1]
    mu_pad = ((0, 0), (0, rw_width - rw_proj))
    mu_prev_p = jnp.pad(rwkv_mu_prev, mu_pad)[:, None, :]
    mu_next_p = jnp.pad(rwkv_mu_next, mu_pad)[:, None, :]
    rw_vecs = jnp.stack([rwkv_k_k, rwkv_k_a, rwkv_r_k.reshape(depth, rw_dim)], axis=1)
    rw_bias = jnp.concatenate([rwkv_w0[:, 0], rwkv_a0[:, 0], rwkv_w0[:, 1], rwkv_a0[:, 1],
                               jnp.zeros((depth, rw_dim), F32)], axis=-1)[:, None, :]
    rw_lora = jnp.zeros((depth, tail_w, 5 * rw_dim), F32)
    for d in range(2):
        rw_lora = rw_lora.at[:, :lora_w, 2 * d * rw_dim:(2 * d + 1) * rw_dim].set(rwkv_w2[:, d])
        rw_lora = rw_lora.at[:, lora_w:lora_w + lora_a, (2 * d + 1) * rw_dim:(2 * d + 2) * rw_dim].set(rwkv_a2[:, d])
    rw_lora = rw_lora.at[:, lora_w + lora_a:lora_w + lora_a + lora_g, 4 * rw_dim:].set(rwkv_g2).astype(BF16)
    rw_ln = jnp.stack([rwkv_ln_w, rwkv_ln_b], axis=1)
    head_id = jnp.arange(rw_dim) // RWKV_HEAD
    ones_bd = (head_id[:, None] == head_id[None, :]).astype(BF16)

    pos = jnp.arange(S)
    att_cos, att_sin = _rope_tables([_rope_angles(pos // GRID_W, ATT_HEAD // 2),
                                     _rope_angles(pos % GRID_W, ATT_HEAD // 2)], B * NC, B)
    ret_cos, ret_sin = _rope_tables([_rope_angles(pos, ret_qk)], B * NC, B)

    silu_all = jnp.zeros((8, D), F32).at[:B].set(jax.nn.silu(c)).at[B].set(jax.nn.silu(c_ctx))

    h = jnp.concatenate([x.reshape(NX, D), ctx.reshape(B * NC, D)], axis=0)
    for l in range(depth):
        ctx_out = l < depth - 1
        n_rows = R if ctx_out else NX
        tm_rows = _pick(n_rows, (1088, 1024, 512, 256))
        mod = matmul(silu_all, w_ada, layer=l, tm=8, tn=1024) + b_ada[l][None, :]
        n1 = norm_mod(h, norm1_w, mod, 0, 1, layer=l, rows=R, n_batch=B, seq=S)
        p_rw = matmul(n1, w_in_rw, layer=l, tn=512)
        p_ret = matmul(n1, w_in_ret, layer=l, tn=_pick(ret_proj, (1024, 768, 512)))
        p_att = matmul(n1, w_in_att, layer=l, tn=_pick(att_proj, (1024, 768, 512)))
        p_gate = matmul(n1, w_in_gate, layer=l, rows=n_rows, tm=tm_rows, tn=1024)

        r_, v_, kk, lw, kd, a_sig, bonus, rw_gate = rwkv_prep(
            p_rw, mu_prev_p, mu_next_p, rw_vecs, rw_bias, rw_lora, ones_bd, layer=l, width=rw_width,
            dim=rw_dim, lora_w=lora_w, lora_a=lora_a, n_x=NX, seq=S, n_ctx=NC)
        y2 = rwkv_scan(r_, v_, kk, lw, kd, a_sig, n_batch=B, seq=S, n_ctx=NC)
        rw_out = rwkv_finish(y2, bonus, rw_gate, rw_ln, ones_bd, layer=l)

        o2 = retention_scan(p_ret, ret_heads, ret_cos, ret_sin, ret_log_decay[l], n_batch=B, seq=S, n_ctx=NC)
        rt_out = retention_finish(o2, p_ret, 2 * nqk + ret_dim)

        ka, vta = attention_kv_prep(p_att, att_dim, att_kv * ATT_HEAD, attn_k_norm, att_cos, att_sin, layer=l)
        att_args = dict(layer=l, n_batch=B)
        at_out = gqa_attention(p_att, attn_q_norm, att_cos, att_sin, ka, vta, q_row0=0, q_len=S,
                               segs=((NX, NC), (0, S)), **att_args)
        if ctx_out:
            at_c = gqa_attention(p_att, attn_q_norm, att_cos, att_sin, ka, vta, q_row0=NX, q_len=NC,
                                 segs=((NX, NC),), **att_args)
            at_out = jnp.concatenate([at_out, at_c], axis=0)

        m = merge_branches(rw_out, rt_out, at_out, w_ro_bf, w_to_bf, w_ao_bf, p_gate, 0,
                           layer=l, rows=n_rows, tm=tm_rows)
        h1 = matmul_residual(m, w_out_bf, h, mod, 2, layer=l, n_batch=B, seq=S, tm=tm_rows, tn=512)
        n2 = norm_mod(h1, norm2_w, mod, 3, 4, layer=l, rows=n_rows, n_batch=B, seq=S)
        act = ffn_up_conv_gate(n2, w_up_bf, ffn_conv_w, conv_b3, layer=l, n_x=NX, seq=S, n_ctx=NC, tm=tm_rows, tn=256)
        h = matmul_residual(act, w_down_bf, h1, mod, 5, layer=l, n_batch=B, seq=S, tm=tm_rows, tn=512,
                            tk=ffn_dim // 2)
    return h.reshape(B, S, D)
```

```python
import functools

import jax
import jax.numpy as jnp
from jax import lax
from jax.experimental import pallas as pl
from jax.experimental.pallas import tpu as pltpu

F32 = jnp.float32
BF16 = jnp.bfloat16

V7X_VMEM_LIMIT_CAP = 56 * 1024 * 1024
LANE = 128
BF16_SUBLANES = 16

GRID_W = 64
RWKV_HEAD = 64
RWKV_LN_EPS = 64e-5
RWKV_CHUNK = 64
RET_V_HEAD = 256
RET_CHUNK = 128
RET_NORM_EPS = 1e-6
ATT_HEAD = 128
ATT_GROUP = 4
ROPE_THETA = 10000.0
NORM_EPS = 1e-6
LOG2_E = 1.4426950408889634

_NT = (((1,), (1,)), ((), ()))
_TN = (((0,), (0,)), ((), ()))


def _vmem_limit(tile_bytes):
    return int(min(max(2 * tile_bytes + (8 << 20), 16 << 20), V7X_VMEM_LIMIT_CAP))


def _dot(x, y, dims=(((1,), (0,)), ((), ()))):
    return lax.dot_general(x.astype(BF16), y.astype(BF16), dims, preferred_element_type=F32)


def _split_bf16(x):
    hi = x.astype(BF16)
    return hi, (x - hi.astype(F32)).astype(BF16)


def _dot3(x, y):
    xh, xl = _split_bf16(x)
    yh, yl = _split_bf16(y)
    return jnp.dot(jnp.concatenate([xh, xh, xl], axis=1), jnp.concatenate([yh, yl, yh], axis=0),
                   preferred_element_type=F32)


def _mm_kernel(a_ref, b_ref, o_ref, *, nk):
    part = _dot(a_ref[...], b_ref[...])
    if nk == 1:
        o_ref[...] = part.astype(o_ref.dtype)
    else:
        k = pl.program_id(2)

        @pl.when(k == 0)
        def _():
            o_ref[...] = part

        @pl.when(k > 0)
        def _():
            o_ref[...] += part


def _pick(n, prefs):
    for p in prefs:
        if n % p == 0:
            return p
    return n


def matmul(a, b, *, layer=None, rows=None, tm=None, tn=None, tk=None, out_dtype=F32):
    M, K = a.shape
    M = rows or M
    N = b.shape[-1]
    assert b.shape[-2] == K
    tm = tm or _pick(M, (1088, 1024, 512, 256, 128, 8))
    tn = tn or _pick(N, (1024, 640, 512, 256, 128))
    tk = tk or K
    assert M % tm == 0 and N % tn == 0 and K % tk == 0, (M, N, K, tm, tn, tk)
    nk = K // tk
    if nk > 1:
        assert out_dtype == F32
    if b.ndim == 3:
        b_spec = pl.BlockSpec((None, tk, tn), lambda i, j, k: (layer, k, j))
    else:
        b_spec = pl.BlockSpec((tk, tn), lambda i, j, k: (k, j))
    tile_bytes = (tm * tk * a.dtype.itemsize + tk * tn * (b.dtype.itemsize + 2)
                  + tm * tn * (jnp.dtype(out_dtype).itemsize + 4))
    return pl.pallas_call(
        functools.partial(_mm_kernel, nk=nk),
        grid=(M // tm, N // tn, nk),
        in_specs=[pl.BlockSpec((tm, tk), lambda i, j, k: (i, k)), b_spec],
        out_specs=pl.BlockSpec((tm, tn), lambda i, j, k: (i, j)),
        out_shape=jax.ShapeDtypeStruct((M, N), out_dtype),
        compiler_params=pltpu.CompilerParams(
            dimension_semantics=("parallel", "parallel", "arbitrary"),
            vmem_limit_bytes=_vmem_limit(tile_bytes)),
        name="matmul",
    )(a, b)


def _row_ids(row0, tm):
    return row0 + lax.broadcasted_iota(jnp.int32, (tm, 1), 0)


def _seq_edges(g, n_x, seq, n_ctx):
    in_x = g < n_x
    starts = jnp.where(in_x, g & (seq - 1), (g - n_x) & (n_ctx - 1)) == 0
    ends = jnp.where(in_x, (g + 1) & (seq - 1), (g + 1 - n_x) & (n_ctx - 1)) == 0
    return starts, ends


def _row_select(table, g, n_batch, seq):
    out = table[n_batch:n_batch + 1]
    seq_id = g >> (seq.bit_length() - 1)
    for b in range(n_batch):
        out = jnp.where(seq_id == b, table[b:b + 1], out)
    return out


def _head_sum(x, ones_bd):
    hi, lo = _split_bf16(x)
    return (jnp.dot(hi, ones_bd, preferred_element_type=F32)
            + jnp.dot(lo, ones_bd, preferred_element_type=F32))


def _mm_res_kernel(a_ref, b_ref, h_ref, g_ref, o_ref, *, nk, tm, n_batch, seq):
    part = _dot(a_ref[...], b_ref[...])
    k = pl.program_id(2)
    row0 = pl.program_id(0) * tm

    def finish(acc):
        gate = _row_select(g_ref[...], _row_ids(row0, tm), n_batch, seq)
        o_ref[...] = h_ref[...] + gate * acc

    if nk == 1:
        finish(part)
    else:
        @pl.when(k == 0)
        def _():
            o_ref[...] = part

        @pl.when((k > 0) & (k < nk - 1))
        def _():
            o_ref[...] += part

        @pl.when(k == nk - 1)
        def _():
            finish(o_ref[...] + part)


def matmul_residual(a, b, h, mod, chunk, *, layer, n_batch, seq, tm, tn, tk=None):
    M, K = a.shape
    N = b.shape[-1]
    tk = tk or K
    assert M % tm == 0 and N % tn == 0 and K % tk == 0
    nk = K // tk
    nn = N // tn
    tile_bytes = tm * tk * 2 + tk * tn * 2 + 3 * tm * tn * 4
    return pl.pallas_call(
        functools.partial(_mm_res_kernel, nk=nk, tm=tm, n_batch=n_batch, seq=seq),
        grid=(M // tm, nn, nk),
        in_specs=[pl.BlockSpec((tm, tk), lambda i, j, k: (i, k)),
                  pl.BlockSpec((None, tk, tn), lambda i, j, k: (layer, k, j)),
                  pl.BlockSpec((tm, tn), lambda i, j, k: (i, j)),
                  pl.BlockSpec((8, tn), lambda i, j, k: (0, chunk * nn + j))],
        out_specs=pl.BlockSpec((tm, tn), lambda i, j, k: (i, j)),
        out_shape=jax.ShapeDtypeStruct((M, N), F32),
        compiler_params=pltpu.CompilerParams(
            dimension_semantics=("parallel", "parallel", "arbitrary"),
            vmem_limit_bytes=_vmem_limit(tile_bytes)),
        name="matmul_residual",
    )(a, b, h, mod)


def _norm_mod_kernel(h_ref, w_ref, sc_ref, sh_ref, o_ref):
    x = h_ref[...]
    gain = w_ref[...] * (1.0 + sc_ref[...])
    y = x * lax.rsqrt(jnp.mean(x * x, axis=-1, keepdims=True) + NORM_EPS) * gain + sh_ref[...]
    o_ref[...] = y.astype(o_ref.dtype)


def norm_mod(h, norm_w, mod, shift_chunk, scale_chunk, *, layer, rows, n_batch, seq, tm=512):
    D = h.shape[1]
    assert rows % tm == 0 and seq % tm == 0

    def table_row(i):
        return jnp.minimum(i * tm // seq, n_batch)

    return pl.pallas_call(
        _norm_mod_kernel,
        grid=(rows // tm,),
        in_specs=[pl.BlockSpec((tm, D), lambda i: (i, 0)),
                  pl.BlockSpec((None, 1, D), lambda i: (layer, 0, 0)),
                  pl.BlockSpec((None, 1, D), lambda i: (table_row(i), 0, scale_chunk)),
                  pl.BlockSpec((None, 1, D), lambda i: (table_row(i), 0, shift_chunk))],
        out_specs=pl.BlockSpec((tm, D), lambda i: (i, 0)),
        out_shape=jax.ShapeDtypeStruct((rows, D), BF16),
        compiler_params=pltpu.CompilerParams(
            dimension_semantics=("parallel",), vmem_limit_bytes=_vmem_limit(tm * D * 10)),
        name="norm_mod",
    )(h, norm_w[:, None, :], mod[:, None, :], mod[:, None, :])


def _ffn_up_kernel(a_ref, ap_ref, an_ref, wv_ref, wg_ref, cv_ref, cg_ref, bv_ref, bg_ref, o_ref, ext_ref,
                   *, tm, halo, n_x, seq, n_ctx):
    i = pl.program_id(0)
    j = pl.program_id(1)

    @pl.when(j == 0)
    def _():
        ext_ref[0:halo, :] = ap_ref[...]
        ext_ref[halo:halo + tm, :] = a_ref[...]
        ext_ref[halo + tm:, :] = an_ref[...]

    starts, ends = _seq_edges(_row_ids(i * tm, tm), n_x, seq, n_ctx)
    a_ext = ext_ref[...]

    def conv(w_ref, c_ref, b_ref):
        u = jnp.dot(a_ext, w_ref[...], preferred_element_type=F32)
        prev = jnp.where(starts, 0.0, u[halo - 1:halo - 1 + tm])
        nxt = jnp.where(ends, 0.0, u[halo + 1:halo + 1 + tm])
        c = c_ref[...]
        return prev * c[0:1] + u[halo:halo + tm] * c[1:2] + nxt * c[2:3] + b_ref[...]

    val = conv(wv_ref, cv_ref, bv_ref)
    gate = conv(wg_ref, cg_ref, bg_ref)
    o_ref[...] = (jax.nn.silu(gate) * val).astype(o_ref.dtype)


def ffn_up_conv_gate(a, w_up, conv_w, conv_b, *, layer, n_x, seq, n_ctx, tm, tn=512):
    R, D = a.shape
    F = w_up.shape[-1] // 2
    halo = BF16_SUBLANES
    assert R % tm == 0 and tm % halo == 0 and F % tn == 0
    assert seq & (seq - 1) == 0 and n_ctx & (n_ctx - 1) == 0
    nf = F // tn
    nh = tm // halo
    last = R // halo - 1
    tile_bytes = (tm * D * 2 + 2 * D * tn * 2 + tm * tn * 2) + ((tm + 2 * halo) * D * 2) // 2 + 4 * tm * tn * 4
    kern = functools.partial(_ffn_up_kernel, tm=tm, halo=halo, n_x=n_x, seq=seq, n_ctx=n_ctx)
    return pl.pallas_call(
        kern,
        grid=(R // tm, nf),
        in_specs=[pl.BlockSpec((tm, D), lambda i, j: (i, 0)),
                  pl.BlockSpec((halo, D), lambda i, j: (jnp.maximum(i * nh - 1, 0), 0)),
                  pl.BlockSpec((halo, D), lambda i, j: (jnp.minimum((i + 1) * nh, last), 0)),
                  pl.BlockSpec((None, D, tn), lambda i, j: (layer, 0, j)),
                  pl.BlockSpec((None, D, tn), lambda i, j: (layer, 0, nf + j)),
                  pl.BlockSpec((None, 3, tn), lambda i, j: (layer, 0, j)),
                  pl.BlockSpec((None, 3, tn), lambda i, j: (layer, 0, nf + j)),
                  pl.BlockSpec((None, 1, tn), lambda i, j: (layer, 0, j)),
                  pl.BlockSpec((None, 1, tn), lambda i, j: (layer, 0, nf + j))],
        out_specs=pl.BlockSpec((tm, tn), lambda i, j: (i, j)),
        out_shape=jax.ShapeDtypeStruct((R, F), BF16),
        scratch_shapes=[pltpu.VMEM((tm + 2 * halo, D), BF16)],
        compiler_params=pltpu.CompilerParams(
            dimension_semantics=("parallel", "arbitrary"),
            vmem_limit_bytes=_vmem_limit(tile_bytes)),
        name="ffn_up_conv_gate",
    )(a, a, a, w_up, w_up, conv_w, conv_w, conv_b, conv_b)


def _merge_kernel(rw_ref, rt_ref, at_ref, w1_ref, w2_ref, w3_ref, g1_ref, g2_ref, g3_ref, o_ref):
    def br(x_ref, w_ref, g_ref):
        return jax.nn.sigmoid(g_ref[...]) * _dot(x_ref[...], w_ref[...])

    m = br(rw_ref, w1_ref, g1_ref) + br(rt_ref, w2_ref, g2_ref) + br(at_ref, w3_ref, g3_ref)
    o_ref[...] = m.astype(o_ref.dtype)


def merge_branches(rw, rt, at, w1, w2, w3, p_all, gate_col0, *, layer, rows, tm, tn=512):
    D = w1.shape[-1]
    assert rows % tm == 0 and D % tn == 0 and gate_col0 % tn == 0
    g0 = gate_col0 // tn
    nd = D // tn
    k1, k2, k3 = rw.shape[1], rt.shape[1], at.shape[1]
    ks = k1 + k2 + k3
    tile_bytes = tm * ks * 2 + ks * tn * 2 + 3 * tm * tn * 4 + tm * tn * 2 + 3 * tm * tn * 4

    def gspec(i):
        return pl.BlockSpec((tm, tn), lambda m, n, i=i: (m, g0 + i * nd + n))

    def wspec(k):
        return pl.BlockSpec((None, k, tn), lambda m, n: (layer, 0, n))

    return pl.pallas_call(
        _merge_kernel,
        grid=(rows // tm, nd),
        in_specs=[pl.BlockSpec((tm, k1), lambda m, n: (m, 0)),
                  pl.BlockSpec((tm, k2), lambda m, n: (m, 0)),
                  pl.BlockSpec((tm, k3), lambda m, n: (m, 0)),
                  wspec(k1), wspec(k2), wspec(k3),
                  gspec(0), gspec(1), gspec(2)],
        out_specs=pl.BlockSpec((tm, tn), lambda m, n: (m, n)),
        out_shape=jax.ShapeDtypeStruct((rows, D), BF16),
        compiler_params=pltpu.CompilerParams(
            dimension_semantics=("parallel", "parallel"),
            vmem_limit_bytes=_vmem_limit(tile_bytes)),
        name="merge_branches",
    )(rw, rt, at, w1, w2, w3, p_all, p_all, p_all)


def _softplus(z):
    return jnp.maximum(z, 0.0) + jnp.log(1.0 + jnp.exp(-jnp.abs(z)))


def _rwkv_prep_kernel(p_ref, pp_ref, pn_ref, mup_ref, mun_ref, vec_ref, bias_ref, lora_ref, ones_ref,
                      r_ref, v_ref, kk_ref, lw_ref, kd_ref, as_ref, bonus_ref, gate_ref,
                      *, tm, dim, lora_w, lora_a, n_x, seq, n_ctx):
    halo = pp_ref.shape[0]
    g = _row_ids(pl.program_id(0) * tm, tm)
    starts, ends = _seq_edges(g, n_x, seq, n_ctx)
    p = p_ref[...]
    ext = jnp.concatenate([pp_ref[...], p, pn_ref[...]], axis=0)
    prev = jnp.where(starts, 0.0, ext[halo - 1:halo - 1 + tm])
    nxt = jnp.where(ends, 0.0, ext[halo + 1:halo + 1 + tm])
    x = p + mup_ref[...] * (prev - p) + mun_ref[...] * (nxt - p)

    r = x[:, :dim]
    k = x[:, dim:2 * dim]
    v = x[:, 2 * dim:3 * dim]
    tail = x[:, 3 * dim:]
    lane = lax.broadcasted_iota(jnp.int32, tail.shape, 1)
    act = jnp.where(lane < lora_w, jnp.tanh(tail), jnp.where(lane < lora_w + lora_a, tail, jax.nn.sigmoid(tail)))
    pre = _dot(act, lora_ref[...]) + bias_ref[...]

    vec = vec_ref[...]
    ones_bd = ones_ref[...]
    kk = k * vec[0:1]
    kk = kk * lax.rsqrt(jnp.maximum(_head_sum(kk * kk, ones_bd), 1e-24))
    r_ref[...] = r
    v_ref[...] = v
    kk_ref[...] = kk
    bonus = jnp.zeros_like(r)
    for d in range(2):
        w_pre = pre[:, 2 * d * dim:(2 * d + 1) * dim]
        a_pre = pre[:, (2 * d + 1) * dim:(2 * d + 2) * dim]
        lw_ref[d] = -jnp.exp(-_softplus(-w_pre) - 0.5)
        a_sig = jax.nn.sigmoid(a_pre)
        kd = k * (1.0 + (a_sig - 1.0) * vec[1:2])
        as_ref[d] = a_sig
        kd_ref[d] = kd
        bonus = bonus + _head_sum(r * kd * vec[2:3], ones_bd)
    bonus_ref[...] = bonus * v
    gate_ref[...] = pre[:, 4 * dim:]


def rwkv_prep(p_all, mu_prev, mu_next, vecs, bias, lora, ones_bd, *, layer, width, dim, lora_w, lora_a,
              n_x, seq, n_ctx, tm=256):
    R = p_all.shape[0]
    halo = 8
    assert R % tm == 0 and tm % halo == 0
    nh = tm // halo
    last = R // halo - 1
    row = pl.BlockSpec((tm, dim), lambda i: (i, 0))
    per_dir = pl.BlockSpec((2, tm, dim), lambda i: (0, i, 0))

    def par(a):
        return pl.BlockSpec((None,) + a.shape[1:], lambda i: (layer,) + (0,) * (a.ndim - 1))

    f = jax.ShapeDtypeStruct((R, dim), F32)
    f2 = jax.ShapeDtypeStruct((2, R, dim), F32)
    kern = functools.partial(_rwkv_prep_kernel, tm=tm, dim=dim, lora_w=lora_w, lora_a=lora_a,
                             n_x=n_x, seq=seq, n_ctx=n_ctx)
    return pl.pallas_call(
        kern,
        grid=(R // tm,),
        in_specs=[pl.BlockSpec((tm, width), lambda i: (i, 0)),
                  pl.BlockSpec((halo, width), lambda i: (jnp.maximum(i * nh - 1, 0), 0)),
                  pl.BlockSpec((halo, width), lambda i: (jnp.minimum((i + 1) * nh, last), 0)),
                  par(mu_prev), par(mu_next), par(vecs), par(bias), par(lora),
                  pl.BlockSpec(ones_bd.shape, lambda i: (0, 0))],
        out_specs=[row, row, row, per_dir, per_dir, per_dir, row, row],
        out_shape=[f, f, f, f2, f2, f2, f, f],
        compiler_params=pltpu.CompilerParams(
            dimension_semantics=("parallel",),
            vmem_limit_bytes=_vmem_limit(tm * (width + 20 * dim) * 4 + lora.shape[1] * lora.shape[2] * 2
                                         + dim * dim * 2)),
        name="rwkv_prep",
    )(p_all, p_all, p_all, mu_prev, mu_next, vecs, bias, lora, ones_bd)


def _rwkv_finish_kernel(y_ref, bonus_ref, gate_ref, ln_ref, ones_ref, o_ref, *, head):
    ones_bd = ones_ref[...]
    y = y_ref[0] + y_ref[1]
    yc = y - _head_sum(y, ones_bd) * (1.0 / head)
    var = _head_sum(yc * yc, ones_bd) * (1.0 / head)
    ln = ln_ref[...]
    out = yc * lax.rsqrt(var + RWKV_LN_EPS) * ln[0:1] + ln[1:2] + bonus_ref[...]
    o_ref[...] = (out * gate_ref[...]).astype(o_ref.dtype)


def rwkv_finish(y2, bonus, gate, ln_wb, ones_bd, *, layer, tm=256):
    _, R, dim = y2.shape
    assert R % tm == 0
    row = pl.BlockSpec((tm, dim), lambda i: (i, 0))
    return pl.pallas_call(
        functools.partial(_rwkv_finish_kernel, head=RWKV_HEAD),
        grid=(R // tm,),
        in_specs=[pl.BlockSpec((2, tm, dim), lambda i: (0, i, 0)), row, row,
                  pl.BlockSpec((None, 2, dim), lambda i: (layer, 0, 0)),
                  pl.BlockSpec(ones_bd.shape, lambda i: (0, 0))],
        out_specs=row,
        out_shape=jax.ShapeDtypeStruct((R, dim), BF16),
        compiler_params=pltpu.CompilerParams(
            dimension_semantics=("parallel",), vmem_limit_bytes=_vmem_limit(tm * dim * 40 + dim * dim * 2)),
        name="rwkv_finish",
    )(y2, bonus, gate, ln_wb, ones_bd)


def _row_block_fn(n_batch, seq, n_ctx, chunk):
    nctx = n_ctx // chunk
    nx = seq // chunk
    coff = n_batch * nx

    def rb(b, d, c):
        fwd = jnp.where(c < nctx, coff + b * nctx + c, b * nx + (c - nctx))
        bwd = jnp.where(c < nctx, coff + b * nctx + (nctx - 1 - c), b * nx + (nctx + nx - 1 - c))
        return jnp.where(d == 0, fwd, bwd)

    return rb, nctx + nx


def _rwkv_kernel(r_ref, v_ref, kk_ref, lw_ref, kd_ref, as_ref, y_ref, s_ref, *, C, npair):
    fwd = pl.program_id(1) == 0

    @pl.when(pl.program_id(2) == 0)
    def _():
        s_ref[...] = jnp.zeros_like(s_ref)

    W = 2 * C
    row = lax.broadcasted_iota(jnp.int32, (C, W), 0)
    lane = lax.broadcasted_iota(jnp.int32, (C, W), 1)
    col = lane & (C - 1)
    ahead = jnp.where(fwd, row - col, col - row)
    strict = ahead > 0
    lower = ahead >= 0
    left = lane < C
    eye = jnp.where(ahead == 0, 1.0, 0.0)
    r0 = lax.broadcasted_iota(jnp.int32, (C, C), 0)
    c0 = lax.broadcasted_iota(jnp.int32, (C, C), 1)
    tri = jnp.where(jnp.where(fwd, r0 - c0, c0 - r0) >= 0, 1.0, 0.0).astype(BF16)
    bd_mask = (lax.broadcasted_iota(jnp.int32, (W, W), 0) < C) == (lax.broadcasted_iota(jnp.int32, (W, W), 1) < C)

    def only(x, first):
        return jnp.where(left, x, 0.0) if first else jnp.where(left, 0.0, x)

    def stack(*xs):
        return jnp.concatenate(xs, axis=0)

    def bdiag(p):
        return stack(only(p, True), only(p, False))

    P = range(npair)
    sls = [slice(p * W, (p + 1) * W) for p in P]
    r = [r_ref[:, sl] for sl in sls]
    v = [v_ref[:, sl] for sl in sls]
    kk = [kk_ref[:, sl] for sl in sls]
    lw = [lw_ref[:, sl] for sl in sls]
    kd = [kd_ref[:, sl] for sl in sls]
    b = [kk[p] * as_ref[:, sls[p]] for p in P]
    s = [s_ref[p] for p in P]

    def cumsum(x):
        l1 = x.astype(BF16)
        e1 = x - l1.astype(F32)
        l2 = e1.astype(BF16)
        l3 = (e1 - l2.astype(F32)).astype(BF16)
        g3 = _dot(tri, jnp.concatenate([l1, l2, l3], axis=1))
        return g3[:, :W] + g3[:, W:2 * W] + g3[:, 2 * W:]

    g_in = [cumsum(lw[p]) for p in P]
    g_last = [jnp.where(fwd, g[C - 1:C, :], g[0:1, :]) for g in g_in]
    e_neg = [jnp.exp(-g) for g in g_in]
    a_t = [-kk[p] * jnp.exp(g_in[p] - lw[p]) for p in P]
    r_t = [r[p] * jnp.exp(g_in[p]) for p in P]
    b_t = [b[p] * e_neg[p] for p in P]
    k_t = [kd[p] * e_neg[p] for p in P]
    e_rem = [jnp.exp(g_last[p] - g_in[p]) for p in P]
    b_h = [b[p] * e_rem[p] for p in P]
    k_h = [kd[p] * e_rem[p] for p in P]

    res_a = [_dot(stack(only(a_t[p], True), only(r_t[p], True)), stack(k_t[p], b_t[p]), _NT) for p in P]
    res_b = [_dot(stack(only(a_t[p], False), only(r_t[p], False)), stack(b_t[p], k_t[p]), _NT) for p in P]
    ak = [jnp.where(strict, jnp.where(left, res_a[p][:C], res_b[p][:C]), 0.0) for p in P]
    ab = [jnp.where(strict, jnp.where(left, res_b[p][:C], res_a[p][:C]), 0.0) for p in P]
    rk = [jnp.where(lower, jnp.where(left, res_a[p][C:], res_b[p][C:]), 0.0) for p in P]
    rb = [jnp.where(lower, jnp.where(left, res_b[p][C:], res_a[p][C:]), 0.0) for p in P]

    inv = [eye + ab[p] for p in P]
    pw = [_dot3(ab[p], bdiag(ab[p])) for p in P]
    n = 2
    while n < C:
        both = [_dot3(stack(inv[p], pw[p]), bdiag(pw[p])) for p in P]
        inv = [inv[p] + both[p][:C] for p in P]
        pw = [both[p][C:] for p in P]
        n *= 2

    a_s = [_dot(stack(a_t[p], r_t[p]), s[p], _NT) for p in P]
    inner = [a_s[p][:C] + _dot(ak[p], stack(only(v[p], True), only(v[p], False))) for p in P]
    u = [_dot(inv[p], stack(only(inner[p], False), only(inner[p], True))) for p in P]
    for p in P:
        y_ref[:, sls[p]] = a_s[p][C:] + _dot(
            jnp.concatenate([rb[p], rk[p]], axis=1),
            stack(only(u[p], False), only(u[p], True), only(v[p], True), only(v[p], False)))
    for p in P:
        upd = _dot(stack(u[p], v[p]), stack(b_h[p], k_h[p]), _TN)
        s_ref[p] = s[p] * jnp.exp(g_last[p]) + jnp.where(bd_mask, upd, 0.0)


def rwkv_scan(r, v, kk, lw, kd, a_sig, *, n_batch, seq, n_ctx, C=RWKV_CHUNK):
    R, HN = r.shape
    npair = HN // (2 * RWKV_HEAD)
    rb, steps = _row_block_fn(n_batch, seq, n_ctx, C)
    shared = pl.BlockSpec((C, HN), lambda b, d, c: (rb(b, d, c), 0))
    per_dir = pl.BlockSpec((None, C, HN), lambda b, d, c: (d, rb(b, d, c), 0))
    return pl.pallas_call(
        functools.partial(_rwkv_kernel, C=C, npair=npair),
        grid=(n_batch, 2, steps),
        in_specs=[shared, shared, shared, per_dir, per_dir, per_dir],
        out_specs=per_dir,
        out_shape=jax.ShapeDtypeStruct((2, R, HN), F32),
        scratch_shapes=[pltpu.VMEM((npair, 2 * RWKV_HEAD, 2 * RWKV_HEAD), F32)],
        compiler_params=pltpu.CompilerParams(dimension_semantics=("parallel", "parallel", "arbitrary")),
        name="rwkv_scan",
    )(r, v, kk, lw, kd, a_sig)


def _ret_kernel(lg_ref, q_ref, k_ref, v_ref, cos_ref, sin_ref, o_ref, s_ref, *, C, H, dk, dv):
    fwd = pl.program_id(1) == 0

    @pl.when(pl.program_id(2) == 0)
    def _():
        s_ref[...] = jnp.zeros_like(s_ref)

    cos = cos_ref[...]
    sin = sin_ref[...]
    row = lax.broadcasted_iota(jnp.int32, (C, C), 0)
    col = lax.broadcasted_iota(jnp.int32, (C, C), 1)
    ahead = jnp.where(fwd, row - col, col - row)
    dist = jnp.maximum(ahead, 0).astype(F32)
    idx = lax.broadcasted_iota(jnp.int32, (C, 1), 0)
    pos = jnp.where(fwd, idx, C - 1 - idx).astype(F32)

    hs = range(H)
    lg = [lg_ref[h][:, 0:1] for h in hs]
    q = [q_ref[:, h * dk:(h + 1) * dk] for h in hs]
    k = [k_ref[:, h * dk:(h + 1) * dk] * (dk ** -0.5) for h in hs]
    q = [x * cos + pltpu.roll(x, dk // 2, 1) * sin for x in q]
    k = [x * cos + pltpu.roll(x, dk // 2, 1) * sin for x in k]
    v = [v_ref[:, h * dv:(h + 1) * dv] for h in hs]
    s = [s_ref[h] for h in hs]
    scores = [_dot(q[h], k[h], _NT) * jnp.where(ahead >= 0, jnp.exp(dist * lg[h]), 0.0) for h in hs]
    carry = [_dot(q[h] * jnp.exp((pos + 1.0) * lg[h]), s[h]) for h in hs]
    for h in hs:
        o_ref[:, h * dv:(h + 1) * dv] = _dot(scores[h], v[h]) + carry[h]
    for h in hs:
        s_ref[h] = s[h] * jnp.exp(C * lg[h]) + _dot(k[h] * jnp.exp((C - 1.0 - pos) * lg[h]), v[h], _TN)


def retention_scan(p_ret, H, cos2, sin2, lg, *, n_batch, seq, n_ctx, C=RET_CHUNK):
    R = p_ret.shape[0]
    dv = RET_V_HEAD
    dk = dv // 2
    rb, steps = _row_block_fn(n_batch, seq, n_ctx, C)
    lg_b = jnp.broadcast_to(lg.astype(F32).reshape(2, H, 1, 1), (2, H, 1, LANE))
    tab = pl.BlockSpec((C, dk), lambda b, d, c: (rb(b, d, c), 0))
    return pl.pallas_call(
        functools.partial(_ret_kernel, C=C, H=H, dk=dk, dv=dv),
        grid=(n_batch, 2, steps),
        in_specs=[pl.BlockSpec((None, H, 1, LANE), lambda b, d, c: (d, 0, 0, 0)),
                  pl.BlockSpec((C, H * dk), lambda b, d, c: (rb(b, d, c), 0)),
                  pl.BlockSpec((C, H * dk), lambda b, d, c: (rb(b, d, c), 1)),
                  pl.BlockSpec((C, H * dv), lambda b, d, c: (rb(b, d, c), 1)),
                  tab, tab],
        out_specs=pl.BlockSpec((None, C, H * dv), lambda b, d, c: (d, rb(b, d, c), 0)),
        out_shape=jax.ShapeDtypeStruct((2, R, H * dv), F32),
        scratch_shapes=[pltpu.VMEM((H, dk, dv), F32)],
        compiler_params=pltpu.CompilerParams(dimension_semantics=("parallel", "parallel", "arbitrary")),
        name="retention_scan",
    )(lg_b, p_ret, p_ret, p_ret, cos2, sin2)


def _ret_finish_kernel(o_ref, g_ref, out_ref):
    o = o_ref[0] + o_ref[1]
    oc = o - jnp.mean(o, axis=-1, keepdims=True)
    y = oc * lax.rsqrt(jnp.mean(oc * oc, axis=-1, keepdims=True) + RET_NORM_EPS)
    out_ref[...] = (y * jax.nn.silu(g_ref[...])).astype(out_ref.dtype)


def retention_finish(o2, p_all, gcol0, *, tm=512):
    _, R, HD = o2.shape
    dv = RET_V_HEAD
    assert R % tm == 0 and gcol0 % dv == 0
    g0 = gcol0 // dv
    return pl.pallas_call(
        _ret_finish_kernel,
        grid=(R // tm, HD // dv),
        in_specs=[pl.BlockSpec((2, tm, dv), lambda i, h: (0, i, h)),
                  pl.BlockSpec((tm, dv), lambda i, h: (i, g0 + h))],
        out_specs=pl.BlockSpec((tm, dv), lambda i, h: (i, h)),
        out_shape=jax.ShapeDtypeStruct((R, HD), BF16),
        compiler_params=pltpu.CompilerParams(dimension_semantics=("parallel", "parallel")),
        name="retention_finish",
    )(o2, p_all)


def _norm_rope(x, w, cos, sin):
    quarter = x.shape[1] // 4
    y = x * lax.rsqrt(jnp.mean(x * x, axis=-1, keepdims=True) + NORM_EPS) * w
    lane = lax.broadcasted_iota(jnp.int32, y.shape, 1)
    first = (lane & (2 * quarter - 1)) < quarter
    swapped = jnp.where(first, pltpu.roll(y, 3 * quarter, 1), pltpu.roll(y, quarter, 1))
    return y * cos + swapped * sin


def _kv_prep_kernel(k_ref, v_ref, w_ref, cos_ref, sin_ref, ko_ref, vo_ref, *, dh):
    cos = cos_ref[...]
    sin = sin_ref[...]
    w = w_ref[...]
    for h in range(k_ref.shape[1] // dh):
        sl = slice(h * dh, (h + 1) * dh)
        ko_ref[:, sl] = _norm_rope(k_ref[:, sl], w, cos, sin).astype(ko_ref.dtype)
        vo_ref[sl, :] = v_ref[:, sl].T.astype(vo_ref.dtype)


def attention_kv_prep(p_att, kcol0, width, k_norm, cos, sin, *, layer, tm=512, dh=ATT_HEAD):
    R = p_att.shape[0]
    assert R % tm == 0 and kcol0 % width == 0
    kb = kcol0 // width
    tab = pl.BlockSpec((tm, dh), lambda i: (i, 0))
    return pl.pallas_call(
        functools.partial(_kv_prep_kernel, dh=dh),
        grid=(R // tm,),
        in_specs=[pl.BlockSpec((tm, width), lambda i: (i, kb)),
                  pl.BlockSpec((tm, width), lambda i: (i, kb + 1)),
                  pl.BlockSpec((None, 1, dh), lambda i: (layer, 0, 0)), tab, tab],
        out_specs=[pl.BlockSpec((tm, width), lambda i: (i, 0)),
                   pl.BlockSpec((width, tm), lambda i: (0, i))],
        out_shape=[jax.ShapeDtypeStruct((R, width), BF16), jax.ShapeDtypeStruct((width, R), BF16)],
        compiler_params=pltpu.CompilerParams(dimension_semantics=("parallel",)),
        name="attention_kv_prep",
    )(p_att, p_att, k_norm[:, None, :], cos, sin)


def _attn_kernel(q_ref, w_ref, cos_ref, sin_ref, *refs, nseg, group, dh):
    k_refs, vt_refs, o_ref = refs[:nseg], refs[nseg:2 * nseg], refs[2 * nseg]
    ks = [r[...] for r in k_refs]
    vts = [r[...] for r in vt_refs]
    cos = cos_ref[...]
    sin = sin_ref[...]
    w = w_ref[...] * (dh ** -0.5 * LOG2_E)

    def scores(g):
        q = _norm_rope(q_ref[:, g * dh:(g + 1) * dh], w, cos, sin).astype(BF16)
        return [lax.dot_general(k, q, _NT, preferred_element_type=F32) for k in ks]

    def softmax(ss):
        m = functools.reduce(jnp.maximum, [jnp.max(s, axis=0, keepdims=True) for s in ss])
        ps = [jnp.exp2(s - m) for s in ss]
        l = functools.reduce(jnp.add, [jnp.sum(p, axis=0, keepdims=True) for p in ps])
        return [p.astype(BF16) for p in ps], l

    def output(g, ps, l):
        ot = functools.reduce(jnp.add, [jnp.dot(vt, p, preferred_element_type=F32)
                                        for p, vt in zip(ps, vts)])
        o_ref[:, g * dh:(g + 1) * dh] = (ot / l).T.astype(o_ref.dtype)

    ss, pl_ = {}, {}
    for step in range(group + 2):
        if step < group:
            ss[step] = scores(step)
        if 0 <= step - 1 < group:
            pl_[step - 1] = softmax(ss.pop(step - 1))
        if 0 <= step - 2 < group:
            output(step - 2, *pl_.pop(step - 2))


def gqa_attention(p_att, q_norm, cos, sin, k, vt, *, layer, n_batch, q_row0, q_len, segs, tq=256,
                  group=ATT_GROUP, dh=ATT_HEAD):
    hkv = k.shape[1] // dh
    gw = group * dh
    assert q_len % tq == 0 and q_row0 % tq == 0
    nq = q_len // tq
    q0 = q_row0 // tq
    for row0, ln in segs:
        assert row0 % ln == 0
    k_specs = [pl.BlockSpec((ln, dh), lambda b, h, i, row0=row0, ln=ln: (row0 // ln + b, h))
               for row0, ln in segs]
    vt_specs = [pl.BlockSpec((dh, ln), lambda b, h, i, row0=row0, ln=ln: (h, row0 // ln + b))
                for row0, ln in segs]
    tk = sum(ln for _, ln in segs)
    tile_bytes = tq * gw * 6 + 2 * tk * dh * 2 + 3 * tq * tk * 4
    tab = pl.BlockSpec((tq, dh), lambda b, h, i: (q0 + b * nq + i, 0))
    return pl.pallas_call(
        functools.partial(_attn_kernel, nseg=len(segs), group=group, dh=dh),
        grid=(n_batch, hkv, nq),
        in_specs=[pl.BlockSpec((tq, gw), lambda b, h, i: (q0 + b * nq + i, h)),
                  pl.BlockSpec((None, 1, dh), lambda b, h, i: (layer, 0, 0)), tab, tab] + k_specs + vt_specs,
        out_specs=pl.BlockSpec((tq, gw), lambda b, h, i: (b * nq + i, h)),
        out_shape=jax.ShapeDtypeStruct((n_batch * q_len, hkv * gw), BF16),
        compiler_params=pltpu.CompilerParams(
            dimension_semantics=("parallel", "parallel", "parallel"),
            vmem_limit_bytes=_vmem_limit(tile_bytes)),
        name="gqa_attention",
    )(p_att, q_norm[:, None, :], cos, sin, *([k] * len(segs)), *([vt] * len(segs)))


def _rope_angles(pos, dim):
    inv_freq = ROPE_THETA ** (-jnp.arange(0, dim, 2, dtype=F32) / dim)
    return pos.astype(F32)[:, None] * inv_freq[None, :]


def _rope_tables(angle_groups, n_pad_rows, reps):
    cos = jnp.concatenate([jnp.concatenate([jnp.cos(a), jnp.cos(a)], axis=1) for a in angle_groups], axis=1)
    sin = jnp.concatenate([jnp.concatenate([-jnp.sin(a), jnp.sin(a)], axis=1) for a in angle_groups], axis=1)
    w = cos.shape[1]
    cos = jnp.concatenate([jnp.tile(cos, (reps, 1)), jnp.ones((n_pad_rows, w), F32)], axis=0)
    sin = jnp.concatenate([jnp.tile(sin, (reps, 1)), jnp.zeros((n_pad_rows, w), F32)], axis=0)
    return cos, sin


def kernel(x, c, ctx, c_ctx, w_ada, b_ada, norm1_w, norm2_w, w_in, rwkv_mu_prev, rwkv_mu_next, rwkv_w0, rwkv_w2, rwkv_a0, rwkv_a2, rwkv_g2, rwkv_k_k, rwkv_k_a, rwkv_r_k, rwkv_ln_w, rwkv_ln_b, w_rwkv_out, ret_log_decay, w_ret_out, attn_q_norm, attn_k_norm, w_attn_out, w_out, w_ffn_up, ffn_conv_w, ffn_conv_b, w_ffn_down):
    B, S, D = x.shape
    NC = ctx.shape[1]
    NX = B * S
    R = NX + B * NC
    depth = w_in.shape[0]
    rw_dim = w_rwkv_out.shape[1]
    rw_heads = rw_dim // RWKV_HEAD
    lora_w = rwkv_w2.shape[2]
    lora_a = rwkv_a2.shape[2]
    rw_proj = rwkv_mu_prev.shape[1]
    ret_dim = w_ret_out.shape[1]
    ret_heads = ret_dim // RET_V_HEAD
    ret_qk = RET_V_HEAD // 2
    nqk = ret_heads * ret_qk
    ret_proj = 2 * nqk + 2 * ret_dim
    att_dim = w_attn_out.shape[1]
    att_heads = att_dim // ATT_HEAD
    att_kv = att_heads // ATT_GROUP
    att_proj = att_dim + 2 * att_kv * ATT_HEAD
    mix_proj = rw_proj + ret_proj + att_proj
    ffn_dim = w_ffn_down.shape[1]

    rw_width = rw_proj + (-rw_proj % 512)
    tail_w = rw_width - 3 * rw_dim
    w_in_rw = jnp.pad(w_in[..., :rw_proj].astype(BF16), ((0, 0), (0, 0), (0, rw_width - rw_proj)))
    w_in_ret = w_in[..., rw_proj:rw_proj + ret_proj].astype(BF16)
    w_in_att = w_in[..., rw_proj + ret_proj:mix_proj].astype(BF16)
    w_in_gate = w_in[..., mix_proj:].astype(BF16)
    w_up_bf = w_ffn_up.astype(BF16)
    w_down_bf = w_ffn_down.astype(BF16)
    w_out_bf = w_out.astype(BF16)
    w_ro_bf, w_to_bf, w_ao_bf = w_rwkv_out.astype(BF16), w_ret_out.astype(BF16), w_attn_out.astype(BF16)
    conv_b3 = ffn_conv_b[:, None, :]

    lora_g = rwkv_g2.shape[1]
    mu_pad = ((0, 0), (0, rw_width - rw_proj))
    mu_prev_p = jnp.pad(rwkv_mu_prev, mu_pad)[:, None, :]
    mu_next_p = jnp.pad(rwkv_mu_next, mu_pad)[:, None, :]
    rw_vecs = jnp.stack([rwkv_k_k, rwkv_k_a, rwkv_r_k.reshape(depth, rw_dim)], axis=1)
    rw_bias = jnp.concatenate([rwkv_w0[:, 0], rwkv_a0[:, 0], rwkv_w0[:, 1], rwkv_a0[:, 1],
                               jnp.zeros((depth, rw_dim), F32)], axis=-1)[:, None, :]
    rw_lora = jnp.zeros((depth, tail_w, 5 * rw_dim), F32)
    for d in range(2):
        rw_lora = rw_lora.at[:, :lora_w, 2 * d * rw_dim:(2 * d + 1) * rw_dim].set(rwkv_w2[:, d])
        rw_lora = rw_lora.at[:, lora_w:lora_w + lora_a, (2 * d + 1) * rw_dim:(2 * d + 2) * rw_dim].set(rwkv_a2[:, d])
    rw_lora = rw_lora.at[:, lora_w + lora_a:lora_w + lora_a + lora_g, 4 * rw_dim:].set(rwkv_g2).astype(BF16)
    rw_ln = jnp.stack([rwkv_ln_w, rwkv_ln_b], axis=1)
    head_id = jnp.arange(rw_dim) // RWKV_HEAD
    ones_bd = (head_id[:, None] == head_id[None, :]).astype(BF16)

    pos = jnp.arange(S)
    att_cos, att_sin = _rope_tables([_rope_angles(pos // GRID_W, ATT_HEAD // 2),
                                     _rope_angles(pos % GRID_W, ATT_HEAD // 2)], B * NC, B)
    ret_cos, ret_sin = _rope_tables([_rope_angles(pos, ret_qk)], B * NC, B)

    silu_all = jnp.zeros((8, D), F32).at[:B].set(jax.nn.silu(c)).at[B].set(jax.nn.silu(c_ctx))

    h = jnp.concatenate([x.reshape(NX, D), ctx.reshape(B * NC, D)], axis=0)
    for l in range(depth):
        ctx_out = l < depth - 1
        n_rows = R if ctx_out else NX
        tm_rows = _pick(n_rows, (1088, 1024, 512, 256))
        mod = matmul(silu_all, w_ada, layer=l, tm=8, tn=1024) + b_ada[l][None, :]
        n1 = norm_mod(h, norm1_w, mod, 0, 1, layer=l, rows=R, n_batch=B, seq=S)
        p_rw = matmul(n1, w_in_rw, layer=l, tn=512)
        p_ret = matmul(n1, w_in_ret, layer=l, tn=_pick(ret_proj, (1024, 768, 512)))
        p_att = matmul(n1, w_in_att, layer=l, tn=_pick(att_proj, (1024, 768, 512)))
        p_gate = matmul(n1, w_in_gate, layer=l, rows=n_rows, tm=tm_rows, tn=1024)

        r_, v_, kk, lw, kd, a_sig, bonus, rw_gate = rwkv_prep(
            p_rw, mu_prev_p, mu_next_p, rw_vecs, rw_bias, rw_lora, ones_bd, layer=l, width=rw_width,
            dim=rw_dim, lora_w=lora_w, lora_a=lora_a, n_x=NX, seq=S, n_ctx=NC)
        y2 = rwkv_scan(r_, v_, kk, lw, kd, a_sig, n_batch=B, seq=S, n_ctx=NC)
        rw_out = rwkv_finish(y2, bonus, rw_gate, rw_ln, ones_bd, layer=l)

        o2 = retention_scan(p_ret, ret_heads, ret_cos, ret_sin, ret_log_decay[l], n_batch=B, seq=S, n_ctx=NC)
        rt_out = retention_finish(o2, p_ret, 2 * nqk + ret_dim)

        ka, vta = attention_kv_prep(p_att, att_dim, att_kv * ATT_HEAD, attn_k_norm, att_cos, att_sin, layer=l)
        att_args = dict(layer=l, n_batch=B)
        at_out = gqa_attention(p_att, attn_q_norm, att_cos, att_sin, ka, vta, q_row0=0, q_len=S,
                               segs=((NX, NC), (0, S)), tq=512, **att_args)
        if ctx_out:
            at_c = gqa_attention(p_att, attn_q_norm, att_cos, att_sin, ka, vta, q_row0=NX, q_len=NC,
                                 segs=((NX, NC),), **att_args)
            at_out = jnp.concatenate([at_out, at_c], axis=0)

        m = merge_branches(rw_out, rt_out, at_out, w_ro_bf, w_to_bf, w_ao_bf, p_gate, 0,
                           layer=l, rows=n_rows, tm=tm_rows)
        h1 = matmul_residual(m, w_out_bf, h, mod, 2, layer=l, n_batch=B, seq=S, tm=tm_rows, tn=512)
        n2 = norm_mod(h1, norm2_w, mod, 3, 4, layer=l, rows=n_rows, n_batch=B, seq=S)
        act = ffn_up_conv_gate(n2, w_up_bf, ffn_conv_w, conv_b3, layer=l, n_x=NX, seq=S, n_ctx=NC, tm=tm_rows, tn=256)
        h = matmul_residual(act, w_down_bf, h1, mod, 5, layer=l, n_batch=B, seq=S, tm=tm_rows, tn=512,
                            tk=ffn_dim // 2)
    return h.reshape(B, S, D)
```

```python
import functools

import jax
import jax.numpy as jnp
from jax import lax
from jax.experimental import pallas as pl
from jax.experimental.pallas import tpu as pltpu

F32 = jnp.float32
BF16 = jnp.bfloat16

V7X_VMEM_LIMIT_CAP = 56 * 1024 * 1024
LANE = 128
BF16_SUBLANES = 16

GRID_W = 64
RWKV_HEAD = 64
RWKV_LN_EPS = 64e-5
RWKV_CHUNK = 64
RWKV_CHUNKS_PER_STEP = 4
RET_V_HEAD = 256
RET_CHUNK = 128
RET_NORM_EPS = 1e-6
ATT_HEAD = 128
ATT_GROUP = 4
ROPE_THETA = 10000.0
NORM_EPS = 1e-6
LOG2_E = 1.4426950408889634

_NT = (((1,), (1,)), ((), ()))
_TN = (((0,), (0,)), ((), ()))


def _vmem_limit(tile_bytes):
    return int(min(max(2 * tile_bytes + (8 << 20), 16 << 20), V7X_VMEM_LIMIT_CAP))


def _dot(x, y, dims=(((1,), (0,)), ((), ()))):
    return lax.dot_general(x.astype(BF16), y.astype(BF16), dims, preferred_element_type=F32)


def _split_bf16(x):
    hi = x.astype(BF16)
    return hi, (x - hi.astype(F32)).astype(BF16)


def _dot3(x, y):
    xh, xl = _split_bf16(x)
    yh, yl = _split_bf16(y)
    return jnp.dot(jnp.concatenate([xh, xh, xl], axis=1), jnp.concatenate([yh, yl, yh], axis=0),
                   preferred_element_type=F32)


def _mm_kernel(a_ref, b_ref, o_ref, *, nk):
    part = _dot(a_ref[...], b_ref[...])
    if nk == 1:
        o_ref[...] = part.astype(o_ref.dtype)
    else:
        k = pl.program_id(2)

        @pl.when(k == 0)
        def _():
            o_ref[...] = part

        @pl.when(k > 0)
        def _():
            o_ref[...] += part


def _pick(n, prefs):
    for p in prefs:
        if n % p == 0:
            return p
    return n


def matmul(a, b, *, layer=None, rows=None, tm=None, tn=None, tk=None, out_dtype=F32):
    M, K = a.shape
    M = rows or M
    N = b.shape[-1]
    assert b.shape[-2] == K
    tm = tm or _pick(M, (1088, 1024, 512, 256, 128, 8))
    tn = tn or _pick(N, (1024, 640, 512, 256, 128))
    tk = tk or K
    assert M % tm == 0 and N % tn == 0 and K % tk == 0, (M, N, K, tm, tn, tk)
    nk = K // tk
    if nk > 1:
        assert out_dtype == F32
    if b.ndim == 3:
        b_spec = pl.BlockSpec((None, tk, tn), lambda i, j, k: (layer, k, j))
    else:
        b_spec = pl.BlockSpec((tk, tn), lambda i, j, k: (k, j))
    tile_bytes = (tm * tk * a.dtype.itemsize + tk * tn * (b.dtype.itemsize + 2)
                  + tm * tn * (jnp.dtype(out_dtype).itemsize + 4))
    return pl.pallas_call(
        functools.partial(_mm_kernel, nk=nk),
        grid=(M // tm, N // tn, nk),
        in_specs=[pl.BlockSpec((tm, tk), lambda i, j, k: (i, k)), b_spec],
        out_specs=pl.BlockSpec((tm, tn), lambda i, j, k: (i, j)),
        out_shape=jax.ShapeDtypeStruct((M, N), out_dtype),
        compiler_params=pltpu.CompilerParams(
            dimension_semantics=("parallel", "parallel", "arbitrary"),
            vmem_limit_bytes=_vmem_limit(tile_bytes)),
        name="matmul",
    )(a, b)


def _row_ids(row0, tm):
    return row0 + lax.broadcasted_iota(jnp.int32, (tm, 1), 0)


def _seq_edges(g, n_x, seq, n_ctx):
    in_x = g < n_x
    starts = jnp.where(in_x, g & (seq - 1), (g - n_x) & (n_ctx - 1)) == 0
    ends = jnp.where(in_x, (g + 1) & (seq - 1), (g + 1 - n_x) & (n_ctx - 1)) == 0
    return starts, ends


def _row_select(table, g, n_batch, seq):
    out = table[n_batch:n_batch + 1]
    seq_id = g >> (seq.bit_length() - 1)
    for b in range(n_batch):
        out = jnp.where(seq_id == b, table[b:b + 1], out)
    return out


def _head_sum(x, ones_bd):
    hi, lo = _split_bf16(x)
    return (jnp.dot(hi, ones_bd, preferred_element_type=F32)
            + jnp.dot(lo, ones_bd, preferred_element_type=F32))


def _mm_res_kernel(a_ref, b_ref, h_ref, g_ref, o_ref, *, nk, tm, n_batch, seq):
    part = _dot(a_ref[...], b_ref[...])
    k = pl.program_id(2)
    row0 = pl.program_id(0) * tm

    def finish(acc):
        gate = _row_select(g_ref[...], _row_ids(row0, tm), n_batch, seq)
        o_ref[...] = h_ref[...] + gate * acc

    if nk == 1:
        finish(part)
    else:
        @pl.when(k == 0)
        def _():
            o_ref[...] = part

        @pl.when((k > 0) & (k < nk - 1))
        def _():
            o_ref[...] += part

        @pl.when(k == nk - 1)
        def _():
            finish(o_ref[...] + part)


def matmul_residual(a, b, h, mod, chunk, *, layer, n_batch, seq, tm, tn, tk=None):
    M, K = a.shape
    N = b.shape[-1]
    tk = tk or K
    assert M % tm == 0 and N % tn == 0 and K % tk == 0
    nk = K // tk
    nn = N // tn
    tile_bytes = tm * tk * 2 + tk * tn * 2 + 3 * tm * tn * 4
    return pl.pallas_call(
        functools.partial(_mm_res_kernel, nk=nk, tm=tm, n_batch=n_batch, seq=seq),
        grid=(M // tm, nn, nk),
        in_specs=[pl.BlockSpec((tm, tk), lambda i, j, k: (i, k)),
                  pl.BlockSpec((None, tk, tn), lambda i, j, k: (layer, k, j)),
                  pl.BlockSpec((tm, tn), lambda i, j, k: (i, j)),
                  pl.BlockSpec((8, tn), lambda i, j, k: (0, chunk * nn + j))],
        out_specs=pl.BlockSpec((tm, tn), lambda i, j, k: (i, j)),
        out_shape=jax.ShapeDtypeStruct((M, N), F32),
        compiler_params=pltpu.CompilerParams(
            dimension_semantics=("parallel", "parallel", "arbitrary"),
            vmem_limit_bytes=_vmem_limit(tile_bytes)),
        name="matmul_residual",
    )(a, b, h, mod)


def _norm_mod_kernel(h_ref, w_ref, sc_ref, sh_ref, o_ref):
    x = h_ref[...]
    gain = w_ref[...] * (1.0 + sc_ref[...])
    y = x * lax.rsqrt(jnp.mean(x * x, axis=-1, keepdims=True) + NORM_EPS) * gain + sh_ref[...]
    o_ref[...] = y.astype(o_ref.dtype)


def norm_mod(h, norm_w, mod, shift_chunk, scale_chunk, *, layer, rows, n_batch, seq, tm=512):
    D = h.shape[1]
    assert rows % tm == 0 and seq % tm == 0

    def table_row(i):
        return jnp.minimum(i * tm // seq, n_batch)

    return pl.pallas_call(
        _norm_mod_kernel,
        grid=(rows // tm,),
        in_specs=[pl.BlockSpec((tm, D), lambda i: (i, 0)),
                  pl.BlockSpec((None, 1, D), lambda i: (layer, 0, 0)),
                  pl.BlockSpec((None, 1, D), lambda i: (table_row(i), 0, scale_chunk)),
                  pl.BlockSpec((None, 1, D), lambda i: (table_row(i), 0, shift_chunk))],
        out_specs=pl.BlockSpec((tm, D), lambda i: (i, 0)),
        out_shape=jax.ShapeDtypeStruct((rows, D), BF16),
        compiler_params=pltpu.CompilerParams(
            dimension_semantics=("parallel",), vmem_limit_bytes=_vmem_limit(tm * D * 10)),
        name="norm_mod",
    )(h, norm_w[:, None, :], mod[:, None, :], mod[:, None, :])


def _ffn_up_kernel(a_ref, ap_ref, an_ref, wv_ref, wg_ref, cv_ref, cg_ref, bv_ref, bg_ref, o_ref, ext_ref,
                   *, tm, halo, n_x, seq, n_ctx):
    i = pl.program_id(0)
    j = pl.program_id(1)

    @pl.when(j == 0)
    def _():
        ext_ref[0:halo, :] = ap_ref[...]
        ext_ref[halo:halo + tm, :] = a_ref[...]
        ext_ref[halo + tm:, :] = an_ref[...]

    starts, ends = _seq_edges(_row_ids(i * tm, tm), n_x, seq, n_ctx)
    a_ext = ext_ref[...]

    def conv(w_ref, c_ref, b_ref):
        u = jnp.dot(a_ext, w_ref[...], preferred_element_type=F32)
        prev = jnp.where(starts, 0.0, u[halo - 1:halo - 1 + tm])
        nxt = jnp.where(ends, 0.0, u[halo + 1:halo + 1 + tm])
        c = c_ref[...]
        return prev * c[0:1] + u[halo:halo + tm] * c[1:2] + nxt * c[2:3] + b_ref[...]

    val = conv(wv_ref, cv_ref, bv_ref)
    gate = conv(wg_ref, cg_ref, bg_ref)
    o_ref[...] = (jax.nn.silu(gate) * val).astype(o_ref.dtype)


def ffn_up_conv_gate(a, w_up, conv_w, conv_b, *, layer, n_x, seq, n_ctx, tm, tn=512):
    R, D = a.shape
    F = w_up.shape[-1] // 2
    halo = BF16_SUBLANES
    assert R % tm == 0 and tm % halo == 0 and F % tn == 0
    assert seq & (seq - 1) == 0 and n_ctx & (n_ctx - 1) == 0
    nf = F // tn
    nh = tm // halo
    last = R // halo - 1
    tile_bytes = (tm * D * 2 + 2 * D * tn * 2 + tm * tn * 2) + ((tm + 2 * halo) * D * 2) // 2 + 4 * tm * tn * 4
    kern = functools.partial(_ffn_up_kernel, tm=tm, halo=halo, n_x=n_x, seq=seq, n_ctx=n_ctx)
    return pl.pallas_call(
        kern,
        grid=(R // tm, nf),
        in_specs=[pl.BlockSpec((tm, D), lambda i, j: (i, 0)),
                  pl.BlockSpec((halo, D), lambda i, j: (jnp.maximum(i * nh - 1, 0), 0)),
                  pl.BlockSpec((halo, D), lambda i, j: (jnp.minimum((i + 1) * nh, last), 0)),
                  pl.BlockSpec((None, D, tn), lambda i, j: (layer, 0, j)),
                  pl.BlockSpec((None, D, tn), lambda i, j: (layer, 0, nf + j)),
                  pl.BlockSpec((None, 3, tn), lambda i, j: (layer, 0, j)),
                  pl.BlockSpec((None, 3, tn), lambda i, j: (layer, 0, nf + j)),
                  pl.BlockSpec((None, 1, tn), lambda i, j: (layer, 0, j)),
                  pl.BlockSpec((None, 1, tn), lambda i, j: (layer, 0, nf + j))],
        out_specs=pl.BlockSpec((tm, tn), lambda i, j: (i, j)),
        out_shape=jax.ShapeDtypeStruct((R, F), BF16),
        scratch_shapes=[pltpu.VMEM((tm + 2 * halo, D), BF16)],
        compiler_params=pltpu.CompilerParams(
            dimension_semantics=("parallel", "arbitrary"),
            vmem_limit_bytes=_vmem_limit(tile_bytes)),
        name="ffn_up_conv_gate",
    )(a, a, a, w_up, w_up, conv_w, conv_w, conv_b, conv_b)


def _merge_kernel(rw_ref, rt_ref, at_ref, w1_ref, w2_ref, w3_ref, g1_ref, g2_ref, g3_ref, o_ref):
    def br(x_ref, w_ref, g_ref):
        return jax.nn.sigmoid(g_ref[...]) * _dot(x_ref[...], w_ref[...])

    m = br(rw_ref, w1_ref, g1_ref) + br(rt_ref, w2_ref, g2_ref) + br(at_ref, w3_ref, g3_ref)
    o_ref[...] = m.astype(o_ref.dtype)


def merge_branches(rw, rt, at, w1, w2, w3, p_all, gate_col0, *, layer, rows, tm, tn=512):
    D = w1.shape[-1]
    assert rows % tm == 0 and D % tn == 0 and gate_col0 % tn == 0
    g0 = gate_col0 // tn
    nd = D // tn
    k1, k2, k3 = rw.shape[1], rt.shape[1], at.shape[1]
    ks = k1 + k2 + k3
    tile_bytes = tm * ks * 2 + ks * tn * 2 + 3 * tm * tn * 4 + tm * tn * 2 + 3 * tm * tn * 4

    def gspec(i):
        return pl.BlockSpec((tm, tn), lambda m, n, i=i: (m, g0 + i * nd + n))

    def wspec(k):
        return pl.BlockSpec((None, k, tn), lambda m, n: (layer, 0, n))

    return pl.pallas_call(
        _merge_kernel,
        grid=(rows // tm, nd),
        in_specs=[pl.BlockSpec((tm, k1), lambda m, n: (m, 0)),
                  pl.BlockSpec((tm, k2), lambda m, n: (m, 0)),
                  pl.BlockSpec((tm, k3), lambda m, n: (m, 0)),
                  wspec(k1), wspec(k2), wspec(k3),
                  gspec(0), gspec(1), gspec(2)],
        out_specs=pl.BlockSpec((tm, tn), lambda m, n: (m, n)),
        out_shape=jax.ShapeDtypeStruct((rows, D), BF16),
        compiler_params=pltpu.CompilerParams(
            dimension_semantics=("parallel", "parallel"),
            vmem_limit_bytes=_vmem_limit(tile_bytes)),
        name="merge_branches",
    )(rw, rt, at, w1, w2, w3, p_all, p_all, p_all)


def _softplus(z):
    return jnp.maximum(z, 0.0) + jnp.log(1.0 + jnp.exp(-jnp.abs(z)))


def _rwkv_prep_kernel(p_ref, pp_ref, pn_ref, mup_ref, mun_ref, vec_ref, bias_ref, lora_ref, ones_ref,
                      r_ref, v_ref, kk_ref, lw_ref, kd_ref, as_ref, bonus_ref, gate_ref,
                      *, tm, dim, lora_w, lora_a, n_x, seq, n_ctx):
    halo = pp_ref.shape[0]
    g = _row_ids(pl.program_id(0) * tm, tm)
    starts, ends = _seq_edges(g, n_x, seq, n_ctx)
    p = p_ref[...]
    ext = jnp.concatenate([pp_ref[...], p, pn_ref[...]], axis=0)
    prev = jnp.where(starts, 0.0, ext[halo - 1:halo - 1 + tm])
    nxt = jnp.where(ends, 0.0, ext[halo + 1:halo + 1 + tm])
    x = p + mup_ref[...] * (prev - p) + mun_ref[...] * (nxt - p)

    r = x[:, :dim]
    k = x[:, dim:2 * dim]
    v = x[:, 2 * dim:3 * dim]
    tail = x[:, 3 * dim:]
    lane = lax.broadcasted_iota(jnp.int32, tail.shape, 1)
    act = jnp.where(lane < lora_w, jnp.tanh(tail), jnp.where(lane < lora_w + lora_a, tail, jax.nn.sigmoid(tail)))
    pre = _dot(act, lora_ref[...]) + bias_ref[...]

    vec = vec_ref[...]
    ones_bd = ones_ref[...]
    kk = k * vec[0:1]
    kk = kk * lax.rsqrt(jnp.maximum(_head_sum(kk * kk, ones_bd), 1e-24))
    r_ref[...] = r
    v_ref[...] = v
    kk_ref[...] = kk
    bonus = jnp.zeros_like(r)
    for d in range(2):
        w_pre = pre[:, 2 * d * dim:(2 * d + 1) * dim]
        a_pre = pre[:, (2 * d + 1) * dim:(2 * d + 2) * dim]
        lw_ref[d] = -jnp.exp(-_softplus(-w_pre) - 0.5)
        a_sig = jax.nn.sigmoid(a_pre)
        kd = k * (1.0 + (a_sig - 1.0) * vec[1:2])
        as_ref[d] = a_sig
        kd_ref[d] = kd
        bonus = bonus + _head_sum(r * kd * vec[2:3], ones_bd)
    bonus_ref[...] = bonus * v
    gate_ref[...] = pre[:, 4 * dim:]


def rwkv_prep(p_all, mu_prev, mu_next, vecs, bias, lora, ones_bd, *, layer, width, dim, lora_w, lora_a,
              n_x, seq, n_ctx, tm=256):
    R = p_all.shape[0]
    halo = 8
    assert R % tm == 0 and tm % halo == 0
    nh = tm // halo
    last = R // halo - 1
    row = pl.BlockSpec((tm, dim), lambda i: (i, 0))
    per_dir = pl.BlockSpec((2, tm, dim), lambda i: (0, i, 0))

    def par(a):
        return pl.BlockSpec((None,) + a.shape[1:], lambda i: (layer,) + (0,) * (a.ndim - 1))

    f = jax.ShapeDtypeStruct((R, dim), F32)
    f2 = jax.ShapeDtypeStruct((2, R, dim), F32)
    kern = functools.partial(_rwkv_prep_kernel, tm=tm, dim=dim, lora_w=lora_w, lora_a=lora_a,
                             n_x=n_x, seq=seq, n_ctx=n_ctx)
    return pl.pallas_call(
        kern,
        grid=(R // tm,),
        in_specs=[pl.BlockSpec((tm, width), lambda i: (i, 0)),
                  pl.BlockSpec((halo, width), lambda i: (jnp.maximum(i * nh - 1, 0), 0)),
                  pl.BlockSpec((halo, width), lambda i: (jnp.minimum((i + 1) * nh, last), 0)),
                  par(mu_prev), par(mu_next), par(vecs), par(bias), par(lora),
                  pl.BlockSpec(ones_bd.shape, lambda i: (0, 0))],
        out_specs=[row, row, row, per_dir, per_dir, per_dir, row, row],
        out_shape=[f, f, f, f2, f2, f2, f, f],
        compiler_params=pltpu.CompilerParams(
            dimension_semantics=("parallel",),
            vmem_limit_bytes=_vmem_limit(tm * (width + 20 * dim) * 4 + lora.shape[1] * lora.shape[2] * 2
                                         + dim * dim * 2)),
        name="rwkv_prep",
    )(p_all, p_all, p_all, mu_prev, mu_next, vecs, bias, lora, ones_bd)


def _rwkv_finish_kernel(y_ref, bonus_ref, gate_ref, ln_ref, ones_ref, o_ref, *, head):
    ones_bd = ones_ref[...]
    y = y_ref[0] + y_ref[1]
    yc = y - _head_sum(y, ones_bd) * (1.0 / head)
    var = _head_sum(yc * yc, ones_bd) * (1.0 / head)
    ln = ln_ref[...]
    out = yc * lax.rsqrt(var + RWKV_LN_EPS) * ln[0:1] + ln[1:2] + bonus_ref[...]
    o_ref[...] = (out * gate_ref[...]).astype(o_ref.dtype)


def rwkv_finish(y2, bonus, gate, ln_wb, ones_bd, *, layer, tm=256):
    _, R, dim = y2.shape
    assert R % tm == 0
    row = pl.BlockSpec((tm, dim), lambda i: (i, 0))
    return pl.pallas_call(
        functools.partial(_rwkv_finish_kernel, head=RWKV_HEAD),
        grid=(R // tm,),
        in_specs=[pl.BlockSpec((2, tm, dim), lambda i: (0, i, 0)), row, row,
                  pl.BlockSpec((None, 2, dim), lambda i: (layer, 0, 0)),
                  pl.BlockSpec(ones_bd.shape, lambda i: (0, 0))],
        out_specs=row,
        out_shape=jax.ShapeDtypeStruct((R, dim), BF16),
        compiler_params=pltpu.CompilerParams(
            dimension_semantics=("parallel",), vmem_limit_bytes=_vmem_limit(tm * dim * 40 + dim * dim * 2)),
        name="rwkv_finish",
    )(y2, bonus, gate, ln_wb, ones_bd)


def _row_block_fn(n_batch, seq, n_ctx, chunk):
    nctx = n_ctx // chunk
    nx = seq // chunk
    coff = n_batch * nx

    def rb(b, d, c):
        fwd = jnp.where(c < nctx, coff + b * nctx + c, b * nx + (c - nctx))
        bwd = jnp.where(c < nctx, coff + b * nctx + (nctx - 1 - c), b * nx + (nctx + nx - 1 - c))
        return jnp.where(d == 0, fwd, bwd)

    return rb, nctx + nx


def _rwkv_kernel(r_ref, v_ref, kk_ref, lw_ref, kd_ref, as_ref, y_ref, s_ref, *, C, npair, nsub):
    fwd = pl.program_id(1) == 0

    @pl.when(pl.program_id(2) == 0)
    def _():
        s_ref[...] = jnp.zeros_like(s_ref)

    W = 2 * C
    row = lax.broadcasted_iota(jnp.int32, (C, W), 0)
    lane = lax.broadcasted_iota(jnp.int32, (C, W), 1)
    col = lane & (C - 1)
    ahead = jnp.where(fwd, row - col, col - row)
    strict = ahead > 0
    lower = ahead >= 0
    left = lane < C
    eye = jnp.where(ahead == 0, 1.0, 0.0)
    r0 = lax.broadcasted_iota(jnp.int32, (C, C), 0)
    c0 = lax.broadcasted_iota(jnp.int32, (C, C), 1)
    tri = jnp.where(jnp.where(fwd, r0 - c0, c0 - r0) >= 0, 1.0, 0.0).astype(BF16)
    bd_mask = (lax.broadcasted_iota(jnp.int32, (W, W), 0) < C) == (lax.broadcasted_iota(jnp.int32, (W, W), 1) < C)

    def only(x, first):
        return jnp.where(left, x, 0.0) if first else jnp.where(left, 0.0, x)

    def stack(*xs):
        return jnp.concatenate(xs, axis=0)

    def bdiag(p):
        return stack(only(p, True), only(p, False))

    items = [(q, p) for q in range(nsub) for p in range(npair)]
    P = range(len(items))
    rows = [pl.ds(pl.multiple_of(jnp.where(fwd, q * C, (nsub - 1 - q) * C), C), C) for q in range(nsub)]
    at = [(rows[q], slice(p * W, (p + 1) * W)) for q, p in items]
    r = [r_ref[ix] for ix in at]
    v = [v_ref[ix] for ix in at]
    kk = [kk_ref[ix] for ix in at]
    lw = [lw_ref[ix] for ix in at]
    kd = [kd_ref[ix] for ix in at]
    b = [kk[p] * as_ref[at[p]] for p in P]

    def cumsum(x):
        l1 = x.astype(BF16)
        e1 = x - l1.astype(F32)
        l2 = e1.astype(BF16)
        l3 = (e1 - l2.astype(F32)).astype(BF16)
        g3 = _dot(tri, jnp.concatenate([l1, l2, l3], axis=1))
        return g3[:, :W] + g3[:, W:2 * W] + g3[:, 2 * W:]

    g_in = [cumsum(lw[p]) for p in P]
    g_last = [jnp.where(fwd, g[C - 1:C, :], g[0:1, :]) for g in g_in]
    e_neg = [jnp.exp(-g) for g in g_in]
    a_t = [-kk[p] * jnp.exp(g_in[p] - lw[p]) for p in P]
    r_t = [r[p] * jnp.exp(g_in[p]) for p in P]
    b_t = [b[p] * e_neg[p] for p in P]
    k_t = [kd[p] * e_neg[p] for p in P]
    e_rem = [jnp.exp(g_last[p] - g_in[p]) for p in P]
    b_h = [b[p] * e_rem[p] for p in P]
    k_h = [kd[p] * e_rem[p] for p in P]

    res_a = [_dot(stack(only(a_t[p], True), only(r_t[p], True)), stack(k_t[p], b_t[p]), _NT) for p in P]
    res_b = [_dot(stack(only(a_t[p], False), only(r_t[p], False)), stack(b_t[p], k_t[p]), _NT) for p in P]
    ak = [jnp.where(strict, jnp.where(left, res_a[p][:C], res_b[p][:C]), 0.0) for p in P]
    ab = [jnp.where(strict, jnp.where(left, res_b[p][:C], res_a[p][:C]), 0.0) for p in P]
    rk = [jnp.where(lower, jnp.where(left, res_a[p][C:], res_b[p][C:]), 0.0) for p in P]
    rb = [jnp.where(lower, jnp.where(left, res_b[p][C:], res_a[p][C:]), 0.0) for p in P]

    inv = [eye + ab[p] for p in P]
    pw = [_dot3(ab[p], bdiag(ab[p])) for p in P]
    n = 2
    while n < C:
        both = [_dot3(stack(inv[p], pw[p]), bdiag(pw[p])) for p in P]
        inv = [inv[p] + both[p][:C] for p in P]
        pw = [both[p][C:] for p in P]
        n *= 2

    s = [s_ref[h] for h in range(npair)]
    for q in range(nsub):
        Q = range(q * npair, (q + 1) * npair)
        a_s = {p: _dot(stack(a_t[p], r_t[p]), s[p - Q[0]], _NT) for p in Q}
        inner = {p: a_s[p][:C] + _dot(ak[p], stack(only(v[p], True), only(v[p], False))) for p in Q}
        u = {p: _dot(inv[p], stack(only(inner[p], False), only(inner[p], True))) for p in Q}
        for p in Q:
            y_ref[at[p]] = a_s[p][C:] + _dot(
                jnp.concatenate([rb[p], rk[p]], axis=1),
                stack(only(u[p], False), only(u[p], True), only(v[p], True), only(v[p], False)))
        for p in Q:
            upd = _dot(stack(u[p], v[p]), stack(b_h[p], k_h[p]), _TN)
            s[p - Q[0]] = s[p - Q[0]] * jnp.exp(g_last[p]) + jnp.where(bd_mask, upd, 0.0)
    for h in range(npair):
        s_ref[h] = s[h]


def rwkv_scan(r, v, kk, lw, kd, a_sig, *, n_batch, seq, n_ctx, C=RWKV_CHUNK):
    R, HN = r.shape
    npair = HN // (2 * RWKV_HEAD)
    nsub = RWKV_CHUNKS_PER_STEP
    rb, steps = _row_block_fn(n_batch, seq, n_ctx, nsub * C)
    shared = pl.BlockSpec((nsub * C, HN), lambda b, d, c: (rb(b, d, c), 0))
    per_dir = pl.BlockSpec((None, nsub * C, HN), lambda b, d, c: (d, rb(b, d, c), 0))
    return pl.pallas_call(
        functools.partial(_rwkv_kernel, C=C, npair=npair, nsub=nsub),
        grid=(n_batch, 2, steps),
        in_specs=[shared, shared, shared, per_dir, per_dir, per_dir],
        out_specs=per_dir,
        out_shape=jax.ShapeDtypeStruct((2, R, HN), F32),
        scratch_shapes=[pltpu.VMEM((npair, 2 * RWKV_HEAD, 2 * RWKV_HEAD), F32)],
        compiler_params=pltpu.CompilerParams(dimension_semantics=("parallel", "parallel", "arbitrary")),
        name="rwkv_scan",
    )(r, v, kk, lw, kd, a_sig)


def _ret_kernel(lg_ref, q_ref, k_ref, v_ref, cos_ref, sin_ref, o_ref, s_ref, *, C, H, dk, dv):
    fwd = pl.program_id(1) == 0

    @pl.when(pl.program_id(2) == 0)
    def _():
        s_ref[...] = jnp.zeros_like(s_ref)

    cos = cos_ref[...]
    sin = sin_ref[...]
    row = lax.broadcasted_iota(jnp.int32, (C, C), 0)
    col = lax.broadcasted_iota(jnp.int32, (C, C), 1)
    ahead = jnp.where(fwd, row - col, col - row)
    dist = jnp.maximum(ahead, 0).astype(F32)
    idx = lax.broadcasted_iota(jnp.int32, (C, 1), 0)
    pos = jnp.where(fwd, idx, C - 1 - idx).astype(F32)

    hs = range(H)
    lg = [lg_ref[h][:, 0:1] for h in hs]
    q = [q_ref[:, h * dk:(h + 1) * dk] for h in hs]
    k = [k_ref[:, h * dk:(h + 1) * dk] * (dk ** -0.5) for h in hs]
    q = [x * cos + pltpu.roll(x, dk // 2, 1) * sin for x in q]
    k = [x * cos + pltpu.roll(x, dk // 2, 1) * sin for x in k]
    v = [v_ref[:, h * dv:(h + 1) * dv] for h in hs]
    s = [s_ref[h] for h in hs]
    scores = [_dot(q[h], k[h], _NT) * jnp.where(ahead >= 0, jnp.exp(dist * lg[h]), 0.0) for h in hs]
    carry = [_dot(q[h] * jnp.exp((pos + 1.0) * lg[h]), s[h]) for h in hs]
    for h in hs:
        o_ref[:, h * dv:(h + 1) * dv] = _dot(scores[h], v[h]) + carry[h]
    for h in hs:
        s_ref[h] = s[h] * jnp.exp(C * lg[h]) + _dot(k[h] * jnp.exp((C - 1.0 - pos) * lg[h]), v[h], _TN)


def retention_scan(p_ret, H, cos2, sin2, lg, *, n_batch, seq, n_ctx, C=RET_CHUNK):
    R = p_ret.shape[0]
    dv = RET_V_HEAD
    dk = dv // 2
    rb, steps = _row_block_fn(n_batch, seq, n_ctx, C)
    lg_b = jnp.broadcast_to(lg.astype(F32).reshape(2, H, 1, 1), (2, H, 1, LANE))
    tab = pl.BlockSpec((C, dk), lambda b, d, c: (rb(b, d, c), 0))
    return pl.pallas_call(
        functools.partial(_ret_kernel, C=C, H=H, dk=dk, dv=dv),
        grid=(n_batch, 2, steps),
        in_specs=[pl.BlockSpec((None, H, 1, LANE), lambda b, d, c: (d, 0, 0, 0)),
                  pl.BlockSpec((C, H * dk), lambda b, d, c: (rb(b, d, c), 0)),
                  pl.BlockSpec((C, H * dk), lambda b, d, c: (rb(b, d, c), 1)),
                  pl.BlockSpec((C, H * dv), lambda b, d, c: (rb(b, d, c), 1)),
                  tab, tab],
        out_specs=pl.BlockSpec((None, C, H * dv), lambda b, d, c: (d, rb(b, d, c), 0)),
        out_shape=jax.ShapeDtypeStruct((2, R, H * dv), F32),
        scratch_shapes=[pltpu.VMEM((H, dk, dv), F32)],
        compiler_params=pltpu.CompilerParams(dimension_semantics=("parallel", "parallel", "arbitrary")),
        name="retention_scan",
    )(lg_b, p_ret, p_ret, p_ret, cos2, sin2)


def _ret_finish_kernel(o_ref, g_ref, out_ref):
    o = o_ref[0] + o_ref[1]
    oc = o - jnp.mean(o, axis=-1, keepdims=True)
    y = oc * lax.rsqrt(jnp.mean(oc * oc, axis=-1, keepdims=True) + RET_NORM_EPS)
    out_ref[...] = (y * jax.nn.silu(g_ref[...])).astype(out_ref.dtype)


def retention_finish(o2, p_all, gcol0, *, tm=512):
    _, R, HD = o2.shape
    dv = RET_V_HEAD
    assert R % tm == 0 and gcol0 % dv == 0
    g0 = gcol0 // dv
    return pl.pallas_call(
        _ret_finish_kernel,
        grid=(R // tm, HD // dv),
        in_specs=[pl.BlockSpec((2, tm, dv), lambda i, h: (0, i, h)),
                  pl.BlockSpec((tm, dv), lambda i, h: (i, g0 + h))],
        out_specs=pl.BlockSpec((tm, dv), lambda i, h: (i, h)),
        out_shape=jax.ShapeDtypeStruct((R, HD), BF16),
        compiler_params=pltpu.CompilerParams(dimension_semantics=("parallel", "parallel")),
        name="retention_finish",
    )(o2, p_all)


def _norm_rope(x, w, cos, sin):
    quarter = x.shape[1] // 4
    y = x * lax.rsqrt(jnp.mean(x * x, axis=-1, keepdims=True) + NORM_EPS) * w
    lane = lax.broadcasted_iota(jnp.int32, y.shape, 1)
    first = (lane & (2 * quarter - 1)) < quarter
    swapped = jnp.where(first, pltpu.roll(y, 3 * quarter, 1), pltpu.roll(y, quarter, 1))
    return y * cos + swapped * sin


def _kv_prep_kernel(k_ref, v_ref, w_ref, cos_ref, sin_ref, ko_ref, vo_ref, *, dh):
    cos = cos_ref[...]
    sin = sin_ref[...]
    w = w_ref[...]
    for h in range(k_ref.shape[1] // dh):
        sl = slice(h * dh, (h + 1) * dh)
        ko_ref[:, sl] = _norm_rope(k_ref[:, sl], w, cos, sin).astype(ko_ref.dtype)
        vo_ref[sl, :] = v_ref[:, sl].T.astype(vo_ref.dtype)


def attention_kv_prep(p_att, kcol0, width, k_norm, cos, sin, *, layer, tm=512, dh=ATT_HEAD):
    R = p_att.shape[0]
    assert R % tm == 0 and kcol0 % width == 0
    kb = kcol0 // width
    tab = pl.BlockSpec((tm, dh), lambda i: (i, 0))
    return pl.pallas_call(
        functools.partial(_kv_prep_kernel, dh=dh),
        grid=(R // tm,),
        in_specs=[pl.BlockSpec((tm, width), lambda i: (i, kb)),
                  pl.BlockSpec((tm, width), lambda i: (i, kb + 1)),
                  pl.BlockSpec((None, 1, dh), lambda i: (layer, 0, 0)), tab, tab],
        out_specs=[pl.BlockSpec((tm, width), lambda i: (i, 0)),
                   pl.BlockSpec((width, tm), lambda i: (0, i))],
        out_shape=[jax.ShapeDtypeStruct((R, width), BF16), jax.ShapeDtypeStruct((width, R), BF16)],
        compiler_params=pltpu.CompilerParams(dimension_semantics=("parallel",)),
        name="attention_kv_prep",
    )(p_att, p_att, k_norm[:, None, :], cos, sin)


def _attn_kernel(q_ref, w_ref, cos_ref, sin_ref, *refs, nseg, group, dh):
    k_refs, vt_refs, o_ref = refs[:nseg], refs[nseg:2 * nseg], refs[2 * nseg]
    ks = [r[...] for r in k_refs]
    vts = [r[...] for r in vt_refs]
    cos = cos_ref[...]
    sin = sin_ref[...]
    w = w_ref[...] * (dh ** -0.5 * LOG2_E)

    def scores(g):
        q = _norm_rope(q_ref[:, g * dh:(g + 1) * dh], w, cos, sin).astype(BF16)
        return [lax.dot_general(k, q, _NT, preferred_element_type=F32) for k in ks]

    def softmax(ss):
        m = functools.reduce(jnp.maximum, [jnp.max(s, axis=0, keepdims=True) for s in ss])
        ps = [jnp.exp2(s - m) for s in ss]
        l = functools.reduce(jnp.add, [jnp.sum(p, axis=0, keepdims=True) for p in ps])
        return [p.astype(BF16) for p in ps], l

    def output(g, ps, l):
        ot = functools.reduce(jnp.add, [jnp.dot(vt, p, preferred_element_type=F32)
                                        for p, vt in zip(ps, vts)])
        o_ref[:, g * dh:(g + 1) * dh] = (ot / l).T.astype(o_ref.dtype)

    ss, pl_ = {}, {}
    for step in range(group + 2):
        if step < group:
            ss[step] = scores(step)
        if 0 <= step - 1 < group:
            pl_[step - 1] = softmax(ss.pop(step - 1))
        if 0 <= step - 2 < group:
            output(step - 2, *pl_.pop(step - 2))


def gqa_attention(p_att, q_norm, cos, sin, k, vt, *, layer, n_batch, q_row0, q_len, segs, tq=256,
                  group=ATT_GROUP, dh=ATT_HEAD):
    hkv = k.shape[1] // dh
    gw = group * dh
    assert q_len % tq == 0 and q_row0 % tq == 0
    nq = q_len // tq
    q0 = q_row0 // tq
    for row0, ln in segs:
        assert row0 % ln == 0
    k_specs = [pl.BlockSpec((ln, dh), lambda b, h, i, row0=row0, ln=ln: (row0 // ln + b, h))
               for row0, ln in segs]
    vt_specs = [pl.BlockSpec((dh, ln), lambda b, h, i, row0=row0, ln=ln: (h, row0 // ln + b))
                for row0, ln in segs]
    tk = sum(ln for _, ln in segs)
    tile_bytes = tq * gw * 6 + 2 * tk * dh * 2 + 3 * tq * tk * 4
    tab = pl.BlockSpec((tq, dh), lambda b, h, i: (q0 + b * nq + i, 0))
    return pl.pallas_call(
        functools.partial(_attn_kernel, nseg=len(segs), group=group, dh=dh),
        grid=(n_batch, hkv, nq),
        in_specs=[pl.BlockSpec((tq, gw), lambda b, h, i: (q0 + b * nq + i, h)),
                  pl.BlockSpec((None, 1, dh), lambda b, h, i: (layer, 0, 0)), tab, tab] + k_specs + vt_specs,
        out_specs=pl.BlockSpec((tq, gw), lambda b, h, i: (b * nq + i, h)),
        out_shape=jax.ShapeDtypeStruct((n_batch * q_len, hkv * gw), BF16),
        compiler_params=pltpu.CompilerParams(
            dimension_semantics=("parallel", "parallel", "parallel"),
            vmem_limit_bytes=_vmem_limit(tile_bytes)),
        name="gqa_attention",
    )(p_att, q_norm[:, None, :], cos, sin, *([k] * len(segs)), *([vt] * len(segs)))


def _rope_angles(pos, dim):
    inv_freq = ROPE_THETA ** (-jnp.arange(0, dim, 2, dtype=F32) / dim)
    return pos.astype(F32)[:, None] * inv_freq[None, :]


def _rope_tables(angle_groups, n_pad_rows, reps):
    cos = jnp.concatenate([jnp.concatenate([jnp.cos(a), jnp.cos(a)], axis=1) for a in angle_groups], axis=1)
    sin = jnp.concatenate([jnp.concatenate([-jnp.sin(a), jnp.sin(a)], axis=1) for a in angle_groups], axis=1)
    w = cos.shape[1]
    cos = jnp.concatenate([jnp.tile(cos, (reps, 1)), jnp.ones((n_pad_rows, w), F32)], axis=0)
    sin = jnp.concatenate([jnp.tile(sin, (reps, 1)), jnp.zeros((n_pad_rows, w), F32)], axis=0)
    return cos, sin


def kernel(x, c, ctx, c_ctx, w_ada, b_ada, norm1_w, norm2_w, w_in, rwkv_mu_prev, rwkv_mu_next, rwkv_w0, rwkv_w2, rwkv_a0, rwkv_a2, rwkv_g2, rwkv_k_k, rwkv_k_a, rwkv_r_k, rwkv_ln_w, rwkv_ln_b, w_rwkv_out, ret_log_decay, w_ret_out, attn_q_norm, attn_k_norm, w_attn_out, w_out, w_ffn_up, ffn_conv_w, ffn_conv_b, w_ffn_down):
    B, S, D = x.shape
    NC = ctx.shape[1]
    NX = B * S
    R = NX + B * NC
    depth = w_in.shape[0]
    rw_dim = w_rwkv_out.shape[1]
    rw_heads = rw_dim // RWKV_HEAD
    lora_w = rwkv_w2.shape[2]
    lora_a = rwkv_a2.shape[2]
    rw_proj = rwkv_mu_prev.shape[1]
    ret_dim = w_ret_out.shape[1]
    ret_heads = ret_dim // RET_V_HEAD
    ret_qk = RET_V_HEAD // 2
    nqk = ret_heads * ret_qk
    ret_proj = 2 * nqk + 2 * ret_dim
    att_dim = w_attn_out.shape[1]
    att_heads = att_dim // ATT_HEAD
    att_kv = att_heads // ATT_GROUP
    att_proj = att_dim + 2 * att_kv * ATT_HEAD
    mix_proj = rw_proj + ret_proj + att_proj
    ffn_dim = w_ffn_down.shape[1]

    rw_width = rw_proj + (-rw_proj % 512)
    tail_w = rw_width - 3 * rw_dim
    w_in_rw = jnp.pad(w_in[..., :rw_proj].astype(BF16), ((0, 0), (0, 0), (0, rw_width - rw_proj)))
    w_in_ret = w_in[..., rw_proj:rw_proj + ret_proj].astype(BF16)
    w_in_att = w_in[..., rw_proj + ret_proj:mix_proj].astype(BF16)
    w_in_gate = w_in[..., mix_proj:].astype(BF16)
    w_up_bf = w_ffn_up.astype(BF16)
    w_down_bf = w_ffn_down.astype(BF16)
    w_out_bf = w_out.astype(BF16)
    w_ro_bf, w_to_bf, w_ao_bf = w_rwkv_out.astype(BF16), w_ret_out.astype(BF16), w_attn_out.astype(BF16)
    conv_b3 = ffn_conv_b[:, None, :]

    lora_g = rwkv_g2.shape[1]
    mu_pad = ((0, 0), (0, rw_width - rw_proj))
    mu_prev_p = jnp.pad(rwkv_mu_prev, mu_pad)[:, None, :]
    mu_next_p = jnp.pad(rwkv_mu_next, mu_pad)[:, None, :]
    rw_vecs = jnp.stack([rwkv_k_k, rwkv_k_a, rwkv_r_k.reshape(depth, rw_dim)], axis=1)
    rw_bias = jnp.concatenate([rwkv_w0[:, 0], rwkv_a0[:, 0], rwkv_w0[:, 1], rwkv_a0[:, 1],
                               jnp.zeros((depth, rw_dim), F32)], axis=-1)[:, None, :]
    rw_lora = jnp.zeros((depth, tail_w, 5 * rw_dim), F32)
    for d in range(2):
        rw_lora = rw_lora.at[:, :lora_w, 2 * d * rw_dim:(2 * d + 1) * rw_dim].set(rwkv_w2[:, d])
        rw_lora = rw_lora.at[:, lora_w:lora_w + lora_a, (2 * d + 1) * rw_dim:(2 * d + 2) * rw_dim].set(rwkv_a2[:, d])
    rw_lora = rw_lora.at[:, lora_w + lora_a:lora_w + lora_a + lora_g, 4 * rw_dim:].set(rwkv_g2).astype(BF16)
    rw_ln = jnp.stack([rwkv_ln_w, rwkv_ln_b], axis=1)
    head_id = jnp.arange(rw_dim) // RWKV_HEAD
    ones_bd = (head_id[:, None] == head_id[None, :]).astype(BF16)

    pos = jnp.arange(S)
    att_cos, att_sin = _rope_tables([_rope_angles(pos // GRID_W, ATT_HEAD // 2),
                                     _rope_angles(pos % GRID_W, ATT_HEAD // 2)], B * NC, B)
    ret_cos, ret_sin = _rope_tables([_rope_angles(pos, ret_qk)], B * NC, B)

    silu_all = jnp.zeros((8, D), F32).at[:B].set(jax.nn.silu(c)).at[B].set(jax.nn.silu(c_ctx))

    h = jnp.concatenate([x.reshape(NX, D), ctx.reshape(B * NC, D)], axis=0)
    for l in range(depth):
        ctx_out = l < depth - 1
        n_rows = R if ctx_out else NX
        tm_rows = _pick(n_rows, (1088, 1024, 512, 256))
        mod = matmul(silu_all, w_ada, layer=l, tm=8, tn=1024) + b_ada[l][None, :]
        n1 = norm_mod(h, norm1_w, mod, 0, 1, layer=l, rows=R, n_batch=B, seq=S)
        p_rw = matmul(n1, w_in_rw, layer=l, tn=512)
        p_ret = matmul(n1, w_in_ret, layer=l, tn=_pick(ret_proj, (1024, 768, 512)))
        p_att = matmul(n1, w_in_att, layer=l, tn=_pick(att_proj, (1024, 768, 512)))
        p_gate = matmul(n1, w_in_gate, layer=l, rows=n_rows, tm=tm_rows, tn=1024)

        r_, v_, kk, lw, kd, a_sig, bonus, rw_gate = rwkv_prep(
            p_rw, mu_prev_p, mu_next_p, rw_vecs, rw_bias, rw_lora, ones_bd, layer=l, width=rw_width,
            dim=rw_dim, lora_w=lora_w, lora_a=lora_a, n_x=NX, seq=S, n_ctx=NC)
        y2 = rwkv_scan(r_, v_, kk, lw, kd, a_sig, n_batch=B, seq=S, n_ctx=NC)
        rw_out = rwkv_finish(y2, bonus, rw_gate, rw_ln, ones_bd, layer=l)

        o2 = retention_scan(p_ret, ret_heads, ret_cos, ret_sin, ret_log_decay[l], n_batch=B, seq=S, n_ctx=NC)
        rt_out = retention_finish(o2, p_ret, 2 * nqk + ret_dim)

        ka, vta = attention_kv_prep(p_att, att_dim, att_kv * ATT_HEAD, attn_k_norm, att_cos, att_sin, layer=l)
        att_args = dict(layer=l, n_batch=B)
        at_out = gqa_attention(p_att, attn_q_norm, att_cos, att_sin, ka, vta, q_row0=0, q_len=S,
                               segs=((NX, NC), (0, S)), tq=512, **att_args)
        if ctx_out:
            at_c = gqa_attention(p_att, attn_q_norm, att_cos, att_sin, ka, vta, q_row0=NX, q_len=NC,
                                 segs=((NX, NC),), **att_args)
            at_out = jnp.concatenate([at_out, at_c], axis=0)

        m = merge_branches(rw_out, rt_out, at_out, w_ro_bf, w_to_bf, w_ao_bf, p_gate, 0,
                           layer=l, rows=n_rows, tm=tm_rows)
        h1 = matmul_residual(m, w_out_bf, h, mod, 2, layer=l, n_batch=B, seq=S, tm=tm_rows, tn=512)
        n2 = norm_mod(h1, norm2_w, mod, 3, 4, layer=l, rows=n_rows, n_batch=B, seq=S)
        act = ffn_up_conv_gate(n2, w_up_bf, ffn_conv_w, conv_b3, layer=l, n_x=NX, seq=S, n_ctx=NC, tm=tm_rows, tn=256)
        h = matmul_residual(act, w_down_bf, h1, mod, 5, layer=l, n_batch=B, seq=S, tm=tm_rows, tn=512,
                            tk=ffn_dim // 2)
    return h.reshape(B, S, D)
```

```python
import functools

import jax
import jax.numpy as jnp
from jax import lax
from jax.experimental import pallas as pl
from jax.experimental.pallas import tpu as pltpu

F32 = jnp.float32
BF16 = jnp.bfloat16

V7X_VMEM_LIMIT_CAP = 56 * 1024 * 1024
LANE = 128
BF16_SUBLANES = 16

GRID_W = 64
RWKV_HEAD = 64
RWKV_LN_EPS = 64e-5
RWKV_CHUNK = 64
RWKV_CHUNKS_PER_STEP = 4
RET_V_HEAD = 256
RET_CHUNK = 128
RET_NORM_EPS = 1e-6
ATT_HEAD = 128
ATT_GROUP = 4
ROPE_THETA = 10000.0
NORM_EPS = 1e-6
LOG2_E = 1.4426950408889634

_NT = (((1,), (1,)), ((), ()))
_TN = (((0,), (0,)), ((), ()))


def _vmem_limit(tile_bytes):
    return int(min(max(2 * tile_bytes + (8 << 20), 16 << 20), V7X_VMEM_LIMIT_CAP))


def _dot(x, y, dims=(((1,), (0,)), ((), ()))):
    return lax.dot_general(x.astype(BF16), y.astype(BF16), dims, preferred_element_type=F32)


def _split_bf16(x):
    hi = x.astype(BF16)
    return hi, (x - hi.astype(F32)).astype(BF16)


def _dot3(x, y):
    xh, xl = _split_bf16(x)
    yh, yl = _split_bf16(y)
    return jnp.dot(jnp.concatenate([xh, xh, xl], axis=1), jnp.concatenate([yh, yl, yh], axis=0),
                   preferred_element_type=F32)


def _mm_kernel(a_ref, b_ref, o_ref, *, nk):
    part = _dot(a_ref[...], b_ref[...])
    if nk == 1:
        o_ref[...] = part.astype(o_ref.dtype)
    else:
        k = pl.program_id(2)

        @pl.when(k == 0)
        def _():
            o_ref[...] = part

        @pl.when(k > 0)
        def _():
            o_ref[...] += part


def _pick(n, prefs):
    for p in prefs:
        if n % p == 0:
            return p
    return n


def matmul(a, b, *, layer=None, rows=None, tm=None, tn=None, tk=None, out_dtype=F32):
    M, K = a.shape
    M = rows or M
    N = b.shape[-1]
    assert b.shape[-2] == K
    tm = tm or _pick(M, (1088, 1024, 512, 256, 128, 8))
    tn = tn or _pick(N, (1024, 640, 512, 256, 128))
    tk = tk or K
    assert M % tm == 0 and N % tn == 0 and K % tk == 0, (M, N, K, tm, tn, tk)
    nk = K // tk
    if nk > 1:
        assert out_dtype == F32
    if b.ndim == 3:
        b_spec = pl.BlockSpec((None, tk, tn), lambda i, j, k: (layer, k, j))
    else:
        b_spec = pl.BlockSpec((tk, tn), lambda i, j, k: (k, j))
    tile_bytes = (tm * tk * a.dtype.itemsize + tk * tn * (b.dtype.itemsize + 2)
                  + tm * tn * (jnp.dtype(out_dtype).itemsize + 4))
    return pl.pallas_call(
        functools.partial(_mm_kernel, nk=nk),
        grid=(M // tm, N // tn, nk),
        in_specs=[pl.BlockSpec((tm, tk), lambda i, j, k: (i, k)), b_spec],
        out_specs=pl.BlockSpec((tm, tn), lambda i, j, k: (i, j)),
        out_shape=jax.ShapeDtypeStruct((M, N), out_dtype),
        compiler_params=pltpu.CompilerParams(
            dimension_semantics=("parallel", "parallel", "arbitrary"),
            vmem_limit_bytes=_vmem_limit(tile_bytes)),
        name="matmul",
    )(a, b)


def _row_ids(row0, tm):
    return row0 + lax.broadcasted_iota(jnp.int32, (tm, 1), 0)


def _seq_edges(g, n_x, seq, n_ctx):
    in_x = g < n_x
    starts = jnp.where(in_x, g & (seq - 1), (g - n_x) & (n_ctx - 1)) == 0
    ends = jnp.where(in_x, (g + 1) & (seq - 1), (g + 1 - n_x) & (n_ctx - 1)) == 0
    return starts, ends


def _row_select(table, g, n_batch, seq):
    out = table[n_batch:n_batch + 1]
    seq_id = g >> (seq.bit_length() - 1)
    for b in range(n_batch):
        out = jnp.where(seq_id == b, table[b:b + 1], out)
    return out


def _head_sum(x, ones_bd):
    hi, lo = _split_bf16(x)
    return (jnp.dot(hi, ones_bd, preferred_element_type=F32)
            + jnp.dot(lo, ones_bd, preferred_element_type=F32))


def _mm_res_kernel(a_ref, b_ref, h_ref, g_ref, o_ref, *, nk, tm, n_batch, seq):
    part = _dot(a_ref[...], b_ref[...])
    k = pl.program_id(2)
    row0 = pl.program_id(0) * tm

    def finish(acc):
        gate = _row_select(g_ref[...], _row_ids(row0, tm), n_batch, seq)
        o_ref[...] = h_ref[...] + gate * acc

    if nk == 1:
        finish(part)
    else:
        @pl.when(k == 0)
        def _():
            o_ref[...] = part

        @pl.when((k > 0) & (k < nk - 1))
        def _():
            o_ref[...] += part

        @pl.when(k == nk - 1)
        def _():
            finish(o_ref[...] + part)


def matmul_residual(a, b, h, mod, chunk, *, layer, n_batch, seq, tm, tn, tk=None):
    M, K = a.shape
    N = b.shape[-1]
    tk = tk or K
    assert M % tm == 0 and N % tn == 0 and K % tk == 0
    nk = K // tk
    nn = N // tn
    tile_bytes = tm * tk * 2 + tk * tn * 2 + 3 * tm * tn * 4
    return pl.pallas_call(
        functools.partial(_mm_res_kernel, nk=nk, tm=tm, n_batch=n_batch, seq=seq),
        grid=(M // tm, nn, nk),
        in_specs=[pl.BlockSpec((tm, tk), lambda i, j, k: (i, k)),
                  pl.BlockSpec((None, tk, tn), lambda i, j, k: (layer, k, j)),
                  pl.BlockSpec((tm, tn), lambda i, j, k: (i, j)),
                  pl.BlockSpec((8, tn), lambda i, j, k: (0, chunk * nn + j))],
        out_specs=pl.BlockSpec((tm, tn), lambda i, j, k: (i, j)),
        out_shape=jax.ShapeDtypeStruct((M, N), F32),
        compiler_params=pltpu.CompilerParams(
            dimension_semantics=("parallel", "parallel", "arbitrary"),
            vmem_limit_bytes=_vmem_limit(tile_bytes)),
        name="matmul_residual",
    )(a, b, h, mod)


def _norm_mod_kernel(h_ref, w_ref, sc_ref, sh_ref, o_ref):
    x = h_ref[...]
    gain = w_ref[...] * (1.0 + sc_ref[...])
    y = x * lax.rsqrt(jnp.mean(x * x, axis=-1, keepdims=True) + NORM_EPS) * gain + sh_ref[...]
    o_ref[...] = y.astype(o_ref.dtype)


def norm_mod(h, norm_w, mod, shift_chunk, scale_chunk, *, layer, rows, n_batch, seq, tm=512):
    D = h.shape[1]
    assert rows % tm == 0 and seq % tm == 0

    def table_row(i):
        return jnp.minimum(i * tm // seq, n_batch)

    return pl.pallas_call(
        _norm_mod_kernel,
        grid=(rows // tm,),
        in_specs=[pl.BlockSpec((tm, D), lambda i: (i, 0)),
                  pl.BlockSpec((None, 1, D), lambda i: (layer, 0, 0)),
                  pl.BlockSpec((None, 1, D), lambda i: (table_row(i), 0, scale_chunk)),
                  pl.BlockSpec((None, 1, D), lambda i: (table_row(i), 0, shift_chunk))],
        out_specs=pl.BlockSpec((tm, D), lambda i: (i, 0)),
        out_shape=jax.ShapeDtypeStruct((rows, D), BF16),
        compiler_params=pltpu.CompilerParams(
            dimension_semantics=("parallel",), vmem_limit_bytes=_vmem_limit(tm * D * 10)),
        name="norm_mod",
    )(h, norm_w[:, None, :], mod[:, None, :], mod[:, None, :])


def _ffn_up_kernel(a_ref, ap_ref, an_ref, wv_ref, wg_ref, cv_ref, cg_ref, bv_ref, bg_ref, o_ref, ext_ref,
                   *, tm, halo, n_x, seq, n_ctx, tiles_per_seq):
    i = pl.program_id(0)
    j = pl.program_id(1)

    @pl.when(j == 0)
    def _():
        prev_rows = ap_ref[...]
        next_rows = an_ref[...]
        if tiles_per_seq:
            prev_rows = jnp.where(i % tiles_per_seq == 0, jnp.zeros_like(prev_rows), prev_rows)
            next_rows = jnp.where((i + 1) % tiles_per_seq == 0, jnp.zeros_like(next_rows), next_rows)
        ext_ref[0:halo, :] = prev_rows
        ext_ref[halo:halo + tm, :] = a_ref[...]
        ext_ref[halo + tm:, :] = next_rows

    a_ext = ext_ref[...]

    def conv(w_ref, c_ref, b_ref):
        u = jnp.dot(a_ext, w_ref[...], preferred_element_type=F32)
        prev = u[halo - 1:halo - 1 + tm]
        nxt = u[halo + 1:halo + 1 + tm]
        if not tiles_per_seq:
            starts, ends = _seq_edges(_row_ids(i * tm, tm), n_x, seq, n_ctx)
            prev = jnp.where(starts, 0.0, prev)
            nxt = jnp.where(ends, 0.0, nxt)
        c = c_ref[...]
        return prev * c[0:1] + u[halo:halo + tm] * c[1:2] + nxt * c[2:3] + b_ref[...]

    val = conv(wv_ref, cv_ref, bv_ref)
    gate = conv(wg_ref, cg_ref, bg_ref)
    o_ref[...] = (jax.nn.silu(gate) * val).astype(o_ref.dtype)


def ffn_up_conv_gate(a, w_up, conv_w, conv_b, *, layer, n_x, seq, n_ctx, tm, tn=512):
    R, D = a.shape
    F = w_up.shape[-1] // 2
    halo = BF16_SUBLANES
    assert R % tm == 0 and tm % halo == 0 and F % tn == 0
    assert seq & (seq - 1) == 0 and n_ctx & (n_ctx - 1) == 0
    nf = F // tn
    nh = tm // halo
    last = R // halo - 1
    tile_bytes = (tm * D * 2 + 2 * D * tn * 2 + tm * tn * 2) + ((tm + 2 * halo) * D * 2) // 2 + 4 * tm * tn * 4
    tiles_per_seq = seq // tm if (R == n_x and seq % tm == 0) else 0
    kern = functools.partial(_ffn_up_kernel, tm=tm, halo=halo, n_x=n_x, seq=seq, n_ctx=n_ctx,
                             tiles_per_seq=tiles_per_seq)
    return pl.pallas_call(
        kern,
        grid=(R // tm, nf),
        in_specs=[pl.BlockSpec((tm, D), lambda i, j: (i, 0)),
                  pl.BlockSpec((halo, D), lambda i, j: (jnp.maximum(i * nh - 1, 0), 0)),
                  pl.BlockSpec((halo, D), lambda i, j: (jnp.minimum((i + 1) * nh, last), 0)),
                  pl.BlockSpec((None, D, tn), lambda i, j: (layer, 0, j)),
                  pl.BlockSpec((None, D, tn), lambda i, j: (layer, 0, nf + j)),
                  pl.BlockSpec((None, 3, tn), lambda i, j: (layer, 0, j)),
                  pl.BlockSpec((None, 3, tn), lambda i, j: (layer, 0, nf + j)),
                  pl.BlockSpec((None, 1, tn), lambda i, j: (layer, 0, j)),
                  pl.BlockSpec((None, 1, tn), lambda i, j: (layer, 0, nf + j))],
        out_specs=pl.BlockSpec((tm, tn), lambda i, j: (i, j)),
        out_shape=jax.ShapeDtypeStruct((R, F), BF16),
        scratch_shapes=[pltpu.VMEM((tm + 2 * halo, D), BF16)],
        compiler_params=pltpu.CompilerParams(
            dimension_semantics=("parallel", "arbitrary"),
            vmem_limit_bytes=_vmem_limit(tile_bytes)),
        name="ffn_up_conv_gate",
    )(a, a, a, w_up, w_up, conv_w, conv_w, conv_b, conv_b)


def _merge_kernel(rw_ref, rt_ref, at_ref, w1_ref, w2_ref, w3_ref, g1_ref, g2_ref, g3_ref, o_ref):
    def br(x_ref, w_ref, g_ref):
        return jax.nn.sigmoid(g_ref[...]) * _dot(x_ref[...], w_ref[...])

    m = br(rw_ref, w1_ref, g1_ref) + br(rt_ref, w2_ref, g2_ref) + br(at_ref, w3_ref, g3_ref)
    o_ref[...] = m.astype(o_ref.dtype)


def merge_branches(rw, rt, at, w1, w2, w3, p_all, gate_col0, *, layer, rows, tm, tn=512):
    D = w1.shape[-1]
    assert rows % tm == 0 and D % tn == 0 and gate_col0 % tn == 0
    g0 = gate_col0 // tn
    nd = D // tn
    k1, k2, k3 = rw.shape[1], rt.shape[1], at.shape[1]
    ks = k1 + k2 + k3
    tile_bytes = tm * ks * 2 + ks * tn * 2 + 3 * tm * tn * 4 + tm * tn * 2 + 3 * tm * tn * 4

    def gspec(i):
        return pl.BlockSpec((tm, tn), lambda m, n, i=i: (m, g0 + i * nd + n))

    def wspec(k):
        return pl.BlockSpec((None, k, tn), lambda m, n: (layer, 0, n))

    return pl.pallas_call(
        _merge_kernel,
        grid=(rows // tm, nd),
        in_specs=[pl.BlockSpec((tm, k1), lambda m, n: (m, 0)),
                  pl.BlockSpec((tm, k2), lambda m, n: (m, 0)),
                  pl.BlockSpec((tm, k3), lambda m, n: (m, 0)),
                  wspec(k1), wspec(k2), wspec(k3),
                  gspec(0), gspec(1), gspec(2)],
        out_specs=pl.BlockSpec((tm, tn), lambda m, n: (m, n)),
        out_shape=jax.ShapeDtypeStruct((rows, D), BF16),
        compiler_params=pltpu.CompilerParams(
            dimension_semantics=("parallel", "parallel"),
            vmem_limit_bytes=_vmem_limit(tile_bytes)),
        name="merge_branches",
    )(rw, rt, at, w1, w2, w3, p_all, p_all, p_all)


def _softplus(z):
    return jnp.maximum(z, 0.0) + jnp.log(1.0 + jnp.exp(-jnp.abs(z)))


def _rwkv_prep_kernel(p_ref, pp_ref, pn_ref, mup_ref, mun_ref, vec_ref, bias_ref, lora_ref, ones_ref,
                      r_ref, v_ref, kk_ref, lw_ref, kd_ref, as_ref, bonus_ref, gate_ref,
                      *, tm, dim, lora_w, lora_a, n_x, seq, n_ctx):
    halo = pp_ref.shape[0]
    g = _row_ids(pl.program_id(0) * tm, tm)
    starts, ends = _seq_edges(g, n_x, seq, n_ctx)
    p = p_ref[...]
    ext = jnp.concatenate([pp_ref[...], p, pn_ref[...]], axis=0)
    prev = jnp.where(starts, 0.0, ext[halo - 1:halo - 1 + tm])
    nxt = jnp.where(ends, 0.0, ext[halo + 1:halo + 1 + tm])
    x = p + mup_ref[...] * (prev - p) + mun_ref[...] * (nxt - p)

    r = x[:, :dim]
    k = x[:, dim:2 * dim]
    v = x[:, 2 * dim:3 * dim]
    tail = x[:, 3 * dim:]
    lane = lax.broadcasted_iota(jnp.int32, tail.shape, 1)
    act = jnp.where(lane < lora_w, jnp.tanh(tail), jnp.where(lane < lora_w + lora_a, tail, jax.nn.sigmoid(tail)))
    pre = _dot(act, lora_ref[...]) + bias_ref[...]

    vec = vec_ref[...]
    ones_bd = ones_ref[...]
    kk = k * vec[0:1]
    kk = kk * lax.rsqrt(jnp.maximum(_head_sum(kk * kk, ones_bd), 1e-24))
    r_ref[...] = r
    v_ref[...] = v
    kk_ref[...] = kk
    bonus = jnp.zeros_like(r)
    for d in range(2):
        w_pre = pre[:, 2 * d * dim:(2 * d + 1) * dim]
        a_pre = pre[:, (2 * d + 1) * dim:(2 * d + 2) * dim]
        lw_ref[d] = -jnp.exp(-_softplus(-w_pre) - 0.5)
        a_sig = jax.nn.sigmoid(a_pre)
        kd = k * (1.0 + (a_sig - 1.0) * vec[1:2])
        as_ref[d] = a_sig
        kd_ref[d] = kd
        bonus = bonus + _head_sum(r * kd * vec[2:3], ones_bd)
    bonus_ref[...] = bonus * v
    gate_ref[...] = pre[:, 4 * dim:]


def rwkv_prep(p_all, mu_prev, mu_next, vecs, bias, lora, ones_bd, *, layer, width, dim, lora_w, lora_a,
              n_x, seq, n_ctx, tm=256):
    R = p_all.shape[0]
    halo = 8
    assert R % tm == 0 and tm % halo == 0
    nh = tm // halo
    last = R // halo - 1
    row = pl.BlockSpec((tm, dim), lambda i: (i, 0))
    per_dir = pl.BlockSpec((2, tm, dim), lambda i: (0, i, 0))

    def par(a):
        return pl.BlockSpec((None,) + a.shape[1:], lambda i: (layer,) + (0,) * (a.ndim - 1))

    f = jax.ShapeDtypeStruct((R, dim), F32)
    f2 = jax.ShapeDtypeStruct((2, R, dim), F32)
    kern = functools.partial(_rwkv_prep_kernel, tm=tm, dim=dim, lora_w=lora_w, lora_a=lora_a,
                             n_x=n_x, seq=seq, n_ctx=n_ctx)
    return pl.pallas_call(
        kern,
        grid=(R // tm,),
        in_specs=[pl.BlockSpec((tm, width), lambda i: (i, 0)),
                  pl.BlockSpec((halo, width), lambda i: (jnp.maximum(i * nh - 1, 0), 0)),
                  pl.BlockSpec((halo, width), lambda i: (jnp.minimum((i + 1) * nh, last), 0)),
                  par(mu_prev), par(mu_next), par(vecs), par(bias), par(lora),
                  pl.BlockSpec(ones_bd.shape, lambda i: (0, 0))],
        out_specs=[row, row, row, per_dir, per_dir, per_dir, row, row],
        out_shape=[f, f, f, f2, f2, f2, f, f],
        compiler_params=pltpu.CompilerParams(
            dimension_semantics=("parallel",),
            vmem_limit_bytes=_vmem_limit(tm * (width + 20 * dim) * 4 + lora.shape[1] * lora.shape[2] * 2
                                         + dim * dim * 2)),
        name="rwkv_prep",
    )(p_all, p_all, p_all, mu_prev, mu_next, vecs, bias, lora, ones_bd)


def _rwkv_finish_kernel(y_ref, bonus_ref, gate_ref, ln_ref, ones_ref, o_ref, *, head):
    ones_bd = ones_ref[...]
    y = y_ref[0] + y_ref[1]
    yc = y - _head_sum(y, ones_bd) * (1.0 / head)
    var = _head_sum(yc * yc, ones_bd) * (1.0 / head)
    ln = ln_ref[...]
    out = yc * lax.rsqrt(var + RWKV_LN_EPS) * ln[0:1] + ln[1:2] + bonus_ref[...]
    o_ref[...] = (out * gate_ref[...]).astype(o_ref.dtype)


def rwkv_finish(y2, bonus, gate, ln_wb, ones_bd, *, layer, tm=256):
    _, R, dim = y2.shape
    assert R % tm == 0
    row = pl.BlockSpec((tm, dim), lambda i: (i, 0))
    return pl.pallas_call(
        functools.partial(_rwkv_finish_kernel, head=RWKV_HEAD),
        grid=(R // tm,),
        in_specs=[pl.BlockSpec((2, tm, dim), lambda i: (0, i, 0)), row, row,
                  pl.BlockSpec((None, 2, dim), lambda i: (layer, 0, 0)),
                  pl.BlockSpec(ones_bd.shape, lambda i: (0, 0))],
        out_specs=row,
        out_shape=jax.ShapeDtypeStruct((R, dim), BF16),
        compiler_params=pltpu.CompilerParams(
            dimension_semantics=("parallel",), vmem_limit_bytes=_vmem_limit(tm * dim * 40 + dim * dim * 2)),
        name="rwkv_finish",
    )(y2, bonus, gate, ln_wb, ones_bd)


def _row_block_fn(n_batch, seq, n_ctx, chunk):
    nctx = n_ctx // chunk
    nx = seq // chunk
    coff = n_batch * nx

    def rb(b, d, c):
        fwd = jnp.where(c < nctx, coff + b * nctx + c, b * nx + (c - nctx))
        bwd = jnp.where(c < nctx, coff + b * nctx + (nctx - 1 - c), b * nx + (nctx + nx - 1 - c))
        return jnp.where(d == 0, fwd, bwd)

    return rb, nctx + nx


def _rwkv_kernel(r_ref, v_ref, kk_ref, lw_ref, kd_ref, as_ref, y_ref, s_ref, *, C, npair, nsub):
    fwd = pl.program_id(1) == 0

    @pl.when(pl.program_id(2) == 0)
    def _():
        s_ref[...] = jnp.zeros_like(s_ref)

    W = 2 * C
    row = lax.broadcasted_iota(jnp.int32, (C, W), 0)
    lane = lax.broadcasted_iota(jnp.int32, (C, W), 1)
    col = lane & (C - 1)
    ahead = jnp.where(fwd, row - col, col - row)
    strict = ahead > 0
    lower = ahead >= 0
    left = lane < C
    eye = jnp.where(ahead == 0, 1.0, 0.0)
    r0 = lax.broadcasted_iota(jnp.int32, (C, C), 0)
    c0 = lax.broadcasted_iota(jnp.int32, (C, C), 1)
    tri = jnp.where(jnp.where(fwd, r0 - c0, c0 - r0) >= 0, 1.0, 0.0).astype(BF16)
    bd_mask = (lax.broadcasted_iota(jnp.int32, (W, W), 0) < C) == (lax.broadcasted_iota(jnp.int32, (W, W), 1) < C)

    def only(x, first):
        return jnp.where(left, x, 0.0) if first else jnp.where(left, 0.0, x)

    def stack(*xs):
        return jnp.concatenate(xs, axis=0)

    def bdiag(p):
        return stack(only(p, True), only(p, False))

    items = [(q, p) for q in range(nsub) for p in range(npair)]
    P = range(len(items))
    rows = [pl.ds(pl.multiple_of(jnp.where(fwd, q * C, (nsub - 1 - q) * C), C), C) for q in range(nsub)]
    at = [(rows[q], slice(p * W, (p + 1) * W)) for q, p in items]
    r = [r_ref[ix] for ix in at]
    v = [v_ref[ix] for ix in at]
    kk = [kk_ref[ix] for ix in at]
    lw = [lw_ref[ix] for ix in at]
    kd = [kd_ref[ix] for ix in at]
    b = [kk[p] * as_ref[at[p]] for p in P]

    def cumsum(x):
        l1 = x.astype(BF16)
        e1 = x - l1.astype(F32)
        l2 = e1.astype(BF16)
        l3 = (e1 - l2.astype(F32)).astype(BF16)
        g3 = _dot(tri, jnp.concatenate([l1, l2, l3], axis=1))
        return g3[:, :W] + g3[:, W:2 * W] + g3[:, 2 * W:]

    g_in = [cumsum(lw[p]) for p in P]
    g_last = [jnp.where(fwd, g[C - 1:C, :], g[0:1, :]) for g in g_in]
    e_neg = [jnp.exp(-g) for g in g_in]
    a_t = [-kk[p] * jnp.exp(g_in[p] - lw[p]) for p in P]
    r_t = [r[p] * jnp.exp(g_in[p]) for p in P]
    b_t = [b[p] * e_neg[p] for p in P]
    k_t = [kd[p] * e_neg[p] for p in P]
    e_rem = [jnp.exp(g_last[p] - g_in[p]) for p in P]
    b_h = [b[p] * e_rem[p] for p in P]
    k_h = [kd[p] * e_rem[p] for p in P]

    res_a = [_dot(stack(only(a_t[p], True), only(r_t[p], True)), stack(k_t[p], b_t[p]), _NT) for p in P]
    res_b = [_dot(stack(only(a_t[p], False), only(r_t[p], False)), stack(b_t[p], k_t[p]), _NT) for p in P]
    ak = [jnp.where(strict, jnp.where(left, res_a[p][:C], res_b[p][:C]), 0.0) for p in P]
    ab = [jnp.where(strict, jnp.where(left, res_b[p][:C], res_a[p][:C]), 0.0) for p in P]
    rk = [jnp.where(lower, jnp.where(left, res_a[p][C:], res_b[p][C:]), 0.0) for p in P]
    rb = [jnp.where(lower, jnp.where(left, res_b[p][C:], res_a[p][C:]), 0.0) for p in P]

    inv = [eye + ab[p] for p in P]
    pw = [_dot3(ab[p], bdiag(ab[p])) for p in P]
    n = 2
    while n < C:
        both = [_dot3(stack(inv[p], pw[p]), bdiag(pw[p])) for p in P]
        inv = [inv[p] + both[p][:C] for p in P]
        pw = [both[p][C:] for p in P]
        n *= 2

    s = [s_ref[h] for h in range(npair)]
    for q in range(nsub):
        Q = range(q * npair, (q + 1) * npair)
        a_s = {p: _dot(stack(a_t[p], r_t[p]), s[p - Q[0]], _NT) for p in Q}
        inner = {p: a_s[p][:C] + _dot(ak[p], stack(only(v[p], True), only(v[p], False))) for p in Q}
        u = {p: _dot(inv[p], stack(only(inner[p], False), only(inner[p], True))) for p in Q}
        for p in Q:
            y_ref[at[p]] = a_s[p][C:] + _dot(
                jnp.concatenate([rb[p], rk[p]], axis=1),
                stack(only(u[p], False), only(u[p], True), only(v[p], True), only(v[p], False)))
        for p in Q:
            upd = _dot(stack(u[p], v[p]), stack(b_h[p], k_h[p]), _TN)
            s[p - Q[0]] = s[p - Q[0]] * jnp.exp(g_last[p]) + jnp.where(bd_mask, upd, 0.0)
    for h in range(npair):
        s_ref[h] = s[h]


def rwkv_scan(r, v, kk, lw, kd, a_sig, *, n_batch, seq, n_ctx, C=RWKV_CHUNK):
    R, HN = r.shape
    npair = HN // (2 * RWKV_HEAD)
    nsub = RWKV_CHUNKS_PER_STEP
    rb, steps = _row_block_fn(n_batch, seq, n_ctx, nsub * C)
    shared = pl.BlockSpec((nsub * C, HN), lambda b, d, c: (rb(b, d, c), 0))
    per_dir = pl.BlockSpec((None, nsub * C, HN), lambda b, d, c: (d, rb(b, d, c), 0))
    return pl.pallas_call(
        functools.partial(_rwkv_kernel, C=C, npair=npair, nsub=nsub),
        grid=(n_batch, 2, steps),
        in_specs=[shared, shared, shared, per_dir, per_dir, per_dir],
        out_specs=per_dir,
        out_shape=jax.ShapeDtypeStruct((2, R, HN), F32),
        scratch_shapes=[pltpu.VMEM((npair, 2 * RWKV_HEAD, 2 * RWKV_HEAD), F32)],
        compiler_params=pltpu.CompilerParams(dimension_semantics=("parallel", "parallel", "arbitrary")),
        name="rwkv_scan",
    )(r, v, kk, lw, kd, a_sig)


def _ret_kernel(lg_ref, q_ref, k_ref, v_ref, cos_ref, sin_ref, o_ref, s_ref, *, C, H, dk, dv):
    fwd = pl.program_id(1) == 0

    @pl.when(pl.program_id(2) == 0)
    def _():
        s_ref[...] = jnp.zeros_like(s_ref)

    cos = cos_ref[...]
    sin = sin_ref[...]
    row = lax.broadcasted_iota(jnp.int32, (C, C), 0)
    col = lax.broadcasted_iota(jnp.int32, (C, C), 1)
    ahead = jnp.where(fwd, row - col, col - row)
    dist = jnp.maximum(ahead, 0).astype(F32)
    idx = lax.broadcasted_iota(jnp.int32, (C, 1), 0)
    pos = jnp.where(fwd, idx, C - 1 - idx).astype(F32)

    hs = range(H)
    lg = [lg_ref[h][:, 0:1] for h in hs]
    q = [q_ref[:, h * dk:(h + 1) * dk] for h in hs]
    k = [k_ref[:, h * dk:(h + 1) * dk] * (dk ** -0.5) for h in hs]
    q = [x * cos + pltpu.roll(x, dk // 2, 1) * sin for x in q]
    k = [x * cos + pltpu.roll(x, dk // 2, 1) * sin for x in k]
    v = [v_ref[:, h * dv:(h + 1) * dv] for h in hs]
    s = [s_ref[h] for h in hs]
    scores = [_dot(q[h], k[h], _NT) * jnp.where(ahead >= 0, jnp.exp(dist * lg[h]), 0.0) for h in hs]
    carry = [_dot(q[h] * jnp.exp((pos + 1.0) * lg[h]), s[h]) for h in hs]
    for h in hs:
        o_ref[:, h * dv:(h + 1) * dv] = _dot(scores[h], v[h]) + carry[h]
    for h in hs:
        s_ref[h] = s[h] * jnp.exp(C * lg[h]) + _dot(k[h] * jnp.exp((C - 1.0 - pos) * lg[h]), v[h], _TN)


def retention_scan(p_ret, H, cos2, sin2, lg, *, n_batch, seq, n_ctx, C=RET_CHUNK):
    R = p_ret.shape[0]
    dv = RET_V_HEAD
    dk = dv // 2
    rb, steps = _row_block_fn(n_batch, seq, n_ctx, C)
    lg_b = jnp.broadcast_to(lg.astype(F32).reshape(2, H, 1, 1), (2, H, 1, LANE))
    tab = pl.BlockSpec((C, dk), lambda b, d, c: (rb(b, d, c), 0))
    return pl.pallas_call(
        functools.partial(_ret_kernel, C=C, H=H, dk=dk, dv=dv),
        grid=(n_batch, 2, steps),
        in_specs=[pl.BlockSpec((None, H, 1, LANE), lambda b, d, c: (d, 0, 0, 0)),
                  pl.BlockSpec((C, H * dk), lambda b, d, c: (rb(b, d, c), 0)),
                  pl.BlockSpec((C, H * dk), lambda b, d, c: (rb(b, d, c), 1)),
                  pl.BlockSpec((C, H * dv), lambda b, d, c: (rb(b, d, c), 1)),
                  tab, tab],
        out_specs=pl.BlockSpec((None, C, H * dv), lambda b, d, c: (d, rb(b, d, c), 0)),
        out_shape=jax.ShapeDtypeStruct((2, R, H * dv), F32),
        scratch_shapes=[pltpu.VMEM((H, dk, dv), F32)],
        compiler_params=pltpu.CompilerParams(dimension_semantics=("parallel", "parallel", "arbitrary")),
        name="retention_scan",
    )(lg_b, p_ret, p_ret, p_ret, cos2, sin2)


def _ret_finish_kernel(o_ref, g_ref, out_ref):
    o = o_ref[0] + o_ref[1]
    oc = o - jnp.mean(o, axis=-1, keepdims=True)
    y = oc * lax.rsqrt(jnp.mean(oc * oc, axis=-1, keepdims=True) + RET_NORM_EPS)
    out_ref[...] = (y * jax.nn.silu(g_ref[...])).astype(out_ref.dtype)


def retention_finish(o2, p_all, gcol0, *, tm=512):
    _, R, HD = o2.shape
    dv = RET_V_HEAD
    assert R % tm == 0 and gcol0 % dv == 0
    g0 = gcol0 // dv
    return pl.pallas_call(
        _ret_finish_kernel,
        grid=(R // tm, HD // dv),
        in_specs=[pl.BlockSpec((2, tm, dv), lambda i, h: (0, i, h)),
                  pl.BlockSpec((tm, dv), lambda i, h: (i, g0 + h))],
        out_specs=pl.BlockSpec((tm, dv), lambda i, h: (i, h)),
        out_shape=jax.ShapeDtypeStruct((R, HD), BF16),
        compiler_params=pltpu.CompilerParams(dimension_semantics=("parallel", "parallel")),
        name="retention_finish",
    )(o2, p_all)


def _norm_rope(x, w, cos, sin):
    quarter = x.shape[1] // 4
    y = x * lax.rsqrt(jnp.mean(x * x, axis=-1, keepdims=True) + NORM_EPS) * w
    lane = lax.broadcasted_iota(jnp.int32, y.shape, 1)
    first = (lane & (2 * quarter - 1)) < quarter
    swapped = jnp.where(first, pltpu.roll(y, 3 * quarter, 1), pltpu.roll(y, quarter, 1))
    return y * cos + swapped * sin


def _kv_prep_kernel(k_ref, v_ref, w_ref, cos_ref, sin_ref, ko_ref, vo_ref, *, dh):
    cos = cos_ref[...]
    sin = sin_ref[...]
    w = w_ref[...]
    for h in range(k_ref.shape[1] // dh):
        sl = slice(h * dh, (h + 1) * dh)
        ko_ref[:, sl] = _norm_rope(k_ref[:, sl], w, cos, sin).astype(ko_ref.dtype)
        vo_ref[sl, :] = v_ref[:, sl].T.astype(vo_ref.dtype)


def attention_kv_prep(p_att, kcol0, width, k_norm, cos, sin, *, layer, tm=512, dh=ATT_HEAD):
    R = p_att.shape[0]
    assert R % tm == 0 and kcol0 % width == 0
    kb = kcol0 // width
    tab = pl.BlockSpec((tm, dh), lambda i: (i, 0))
    return pl.pallas_call(
        functools.partial(_kv_prep_kernel, dh=dh),
        grid=(R // tm,),
        in_specs=[pl.BlockSpec((tm, width), lambda i: (i, kb)),
                  pl.BlockSpec((tm, width), lambda i: (i, kb + 1)),
                  pl.BlockSpec((None, 1, dh), lambda i: (layer, 0, 0)), tab, tab],
        out_specs=[pl.BlockSpec((tm, width), lambda i: (i, 0)),
                   pl.BlockSpec((width, tm), lambda i: (0, i))],
        out_shape=[jax.ShapeDtypeStruct((R, width), BF16), jax.ShapeDtypeStruct((width, R), BF16)],
        compiler_params=pltpu.CompilerParams(dimension_semantics=("parallel",)),
        name="attention_kv_prep",
    )(p_att, p_att, k_norm[:, None, :], cos, sin)


def _attn_kernel(q_ref, w_ref, cos_ref, sin_ref, *refs, nseg, group, dh):
    k_refs, vt_refs, o_ref = refs[:nseg], refs[nseg:2 * nseg], refs[2 * nseg]
    ks = [r[...] for r in k_refs]
    vts = [r[...] for r in vt_refs]
    cos = cos_ref[...]
    sin = sin_ref[...]
    w = w_ref[...] * (dh ** -0.5 * LOG2_E)

    def scores(g):
        q = _norm_rope(q_ref[:, g * dh:(g + 1) * dh], w, cos, sin).astype(BF16)
        return [lax.dot_general(k, q, _NT, preferred_element_type=F32) for k in ks]

    def softmax(ss):
        m = functools.reduce(jnp.maximum, [jnp.max(s, axis=0, keepdims=True) for s in ss])
        ps = [jnp.exp2(s - m) for s in ss]
        l = functools.reduce(jnp.add, [jnp.sum(p, axis=0, keepdims=True) for p in ps])
        return [p.astype(BF16) for p in ps], l

    def output(g, ps, l):
        ot = functools.reduce(jnp.add, [jnp.dot(vt, p, preferred_element_type=F32)
                                        for p, vt in zip(ps, vts)])
        o_ref[:, g * dh:(g + 1) * dh] = (ot / l).T.astype(o_ref.dtype)

    ss, pl_ = {}, {}
    for step in range(group + 2):
        if step < group:
            ss[step] = scores(step)
        if 0 <= step - 1 < group:
            pl_[step - 1] = softmax(ss.pop(step - 1))
        if 0 <= step - 2 < group:
            output(step - 2, *pl_.pop(step - 2))


def gqa_attention(p_att, q_norm, cos, sin, k, vt, *, layer, n_batch, q_row0, q_len, segs, tq=256,
                  group=ATT_GROUP, dh=ATT_HEAD):
    hkv = k.shape[1] // dh
    gw = group * dh
    assert q_len % tq == 0 and q_row0 % tq == 0
    nq = q_len // tq
    q0 = q_row0 // tq
    for row0, ln in segs:
        assert row0 % ln == 0
    k_specs = [pl.BlockSpec((ln, dh), lambda b, h, i, row0=row0, ln=ln: (row0 // ln + b, h))
               for row0, ln in segs]
    vt_specs = [pl.BlockSpec((dh, ln), lambda b, h, i, row0=row0, ln=ln: (h, row0 // ln + b))
                for row0, ln in segs]
    tk = sum(ln for _, ln in segs)
    tile_bytes = tq * gw * 6 + 2 * tk * dh * 2 + 3 * tq * tk * 4
    tab = pl.BlockSpec((tq, dh), lambda b, h, i: (q0 + b * nq + i, 0))
    return pl.pallas_call(
        functools.partial(_attn_kernel, nseg=len(segs), group=group, dh=dh),
        grid=(n_batch, hkv, nq),
        in_specs=[pl.BlockSpec((tq, gw), lambda b, h, i: (q0 + b * nq + i, h)),
                  pl.BlockSpec((None, 1, dh), lambda b, h, i: (layer, 0, 0)), tab, tab] + k_specs + vt_specs,
        out_specs=pl.BlockSpec((tq, gw), lambda b, h, i: (b * nq + i, h)),
        out_shape=jax.ShapeDtypeStruct((n_batch * q_len, hkv * gw), BF16),
        compiler_params=pltpu.CompilerParams(
            dimension_semantics=("parallel", "parallel", "parallel"),
            vmem_limit_bytes=_vmem_limit(tile_bytes)),
        name="gqa_attention",
    )(p_att, q_norm[:, None, :], cos, sin, *([k] * len(segs)), *([vt] * len(segs)))


def _rope_angles(pos, dim):
    inv_freq = ROPE_THETA ** (-jnp.arange(0, dim, 2, dtype=F32) / dim)
    return pos.astype(F32)[:, None] * inv_freq[None, :]


def _rope_tables(angle_groups, n_pad_rows, reps):
    cos = jnp.concatenate([jnp.concatenate([jnp.cos(a), jnp.cos(a)], axis=1) for a in angle_groups], axis=1)
    sin = jnp.concatenate([jnp.concatenate([-jnp.sin(a), jnp.sin(a)], axis=1) for a in angle_groups], axis=1)
    w = cos.shape[1]
    cos = jnp.concatenate([jnp.tile(cos, (reps, 1)), jnp.ones((n_pad_rows, w), F32)], axis=0)
    sin = jnp.concatenate([jnp.tile(sin, (reps, 1)), jnp.zeros((n_pad_rows, w), F32)], axis=0)
    return cos, sin


def kernel(x, c, ctx, c_ctx, w_ada, b_ada, norm1_w, norm2_w, w_in, rwkv_mu_prev, rwkv_mu_next, rwkv_w0, rwkv_w2, rwkv_a0, rwkv_a2, rwkv_g2, rwkv_k_k, rwkv_k_a, rwkv_r_k, rwkv_ln_w, rwkv_ln_b, w_rwkv_out, ret_log_decay, w_ret_out, attn_q_norm, attn_k_norm, w_attn_out, w_out, w_ffn_up, ffn_conv_w, ffn_conv_b, w_ffn_down):
    B, S, D = x.shape
    NC = ctx.shape[1]
    NX = B * S
    R = NX + B * NC
    depth = w_in.shape[0]
    rw_dim = w_rwkv_out.shape[1]
    lora_w = rwkv_w2.shape[2]
    lora_a = rwkv_a2.shape[2]
    rw_proj = rwkv_mu_prev.shape[1]
    ret_dim = w_ret_out.shape[1]
    ret_heads = ret_dim // RET_V_HEAD
    ret_qk = RET_V_HEAD // 2
    nqk = ret_heads * ret_qk
    ret_proj = 2 * nqk + 2 * ret_dim
    att_dim = w_attn_out.shape[1]
    att_heads = att_dim // ATT_HEAD
    att_kv = att_heads // ATT_GROUP
    att_proj = att_dim + 2 * att_kv * ATT_HEAD
    mix_proj = rw_proj + ret_proj + att_proj
    ffn_dim = w_ffn_down.shape[1]

    rw_width = rw_proj + (-rw_proj % 512)
    tail_w = rw_width - 3 * rw_dim
    w_in_rw = jnp.pad(w_in[..., :rw_proj].astype(BF16), ((0, 0), (0, 0), (0, rw_width - rw_proj)))
    w_in_ret = w_in[..., rw_proj:rw_proj + ret_proj].astype(BF16)
    w_in_att = w_in[..., rw_proj + ret_proj:mix_proj].astype(BF16)
    w_in_gate = w_in[..., mix_proj:].astype(BF16)
    w_up_bf = w_ffn_up.astype(BF16)
    w_down_bf = w_ffn_down.astype(BF16)
    w_out_bf = w_out.astype(BF16)
    w_ro_bf, w_to_bf, w_ao_bf = w_rwkv_out.astype(BF16), w_ret_out.astype(BF16), w_attn_out.astype(BF16)
    conv_b3 = ffn_conv_b[:, None, :]

    lora_g = rwkv_g2.shape[1]
    mu_pad = ((0, 0), (0, rw_width - rw_proj))
    mu_prev_p = jnp.pad(rwkv_mu_prev, mu_pad)[:, None, :]
    mu_next_p = jnp.pad(rwkv_mu_next, mu_pad)[:, None, :]
    rw_vecs = jnp.stack([rwkv_k_k, rwkv_k_a, rwkv_r_k.reshape(depth, rw_dim)], axis=1)
    rw_bias = jnp.concatenate([rwkv_w0[:, 0], rwkv_a0[:, 0], rwkv_w0[:, 1], rwkv_a0[:, 1],
                               jnp.zeros((depth, rw_dim), F32)], axis=-1)[:, None, :]
    rw_lora = jnp.zeros((depth, tail_w, 5 * rw_dim), F32)
    for d in range(2):
        rw_lora = rw_lora.at[:, :lora_w, 2 * d * rw_dim:(2 * d + 1) * rw_dim].set(rwkv_w2[:, d])
        rw_lora = rw_lora.at[:, lora_w:lora_w + lora_a, (2 * d + 1) * rw_dim:(2 * d + 2) * rw_dim].set(rwkv_a2[:, d])
    rw_lora = rw_lora.at[:, lora_w + lora_a:lora_w + lora_a + lora_g, 4 * rw_dim:].set(rwkv_g2).astype(BF16)
    rw_ln = jnp.stack([rwkv_ln_w, rwkv_ln_b], axis=1)
    head_id = jnp.arange(rw_dim) // RWKV_HEAD
    ones_bd = (head_id[:, None] == head_id[None, :]).astype(BF16)

    pos = jnp.arange(S)
    att_cos, att_sin = _rope_tables([_rope_angles(pos // GRID_W, ATT_HEAD // 2),
                                     _rope_angles(pos % GRID_W, ATT_HEAD // 2)], B * NC, B)
    ret_cos, ret_sin = _rope_tables([_rope_angles(pos, ret_qk)], B * NC, B)

    silu_all = jnp.zeros((8, D), F32).at[:B].set(jax.nn.silu(c)).at[B].set(jax.nn.silu(c_ctx))

    h = jnp.concatenate([x.reshape(NX, D), ctx.reshape(B * NC, D)], axis=0)
    for l in range(depth):
        ctx_out = l < depth - 1
        n_rows = R if ctx_out else NX
        tm_rows = _pick(n_rows, (1088, 1024, 512, 256))
        mod = matmul(silu_all, w_ada, layer=l, tm=8, tn=1024) + b_ada[l][None, :]
        n1 = norm_mod(h, norm1_w, mod, 0, 1, layer=l, rows=R, n_batch=B, seq=S)
        p_rw = matmul(n1, w_in_rw, layer=l, tn=512)
        p_ret = matmul(n1, w_in_ret, layer=l, tn=_pick(ret_proj, (1024, 768, 512)))
        p_att = matmul(n1, w_in_att, layer=l, tn=_pick(att_proj, (1024, 768, 512)))
        p_gate = matmul(n1, w_in_gate, layer=l, rows=n_rows, tm=tm_rows, tn=1024)

        r_, v_, kk, lw, kd, a_sig, bonus, rw_gate = rwkv_prep(
            p_rw, mu_prev_p, mu_next_p, rw_vecs, rw_bias, rw_lora, ones_bd, layer=l, width=rw_width,
            dim=rw_dim, lora_w=lora_w, lora_a=lora_a, n_x=NX, seq=S, n_ctx=NC)
        y2 = rwkv_scan(r_, v_, kk, lw, kd, a_sig, n_batch=B, seq=S, n_ctx=NC)
        rw_out = rwkv_finish(y2, bonus, rw_gate, rw_ln, ones_bd, layer=l)

        o2 = retention_scan(p_ret, ret_heads, ret_cos, ret_sin, ret_log_decay[l], n_batch=B, seq=S, n_ctx=NC)
        rt_out = retention_finish(o2, p_ret, 2 * nqk + ret_dim)

        ka, vta = attention_kv_prep(p_att, att_dim, att_kv * ATT_HEAD, attn_k_norm, att_cos, att_sin, layer=l)
        att_args = dict(layer=l, n_batch=B)
        at_out = gqa_attention(p_att, attn_q_norm, att_cos, att_sin, ka, vta, q_row0=0, q_len=S,
                               segs=((NX, NC), (0, S)), tq=512, **att_args)
        if ctx_out:
            at_c = gqa_attention(p_att, attn_q_norm, att_cos, att_sin, ka, vta, q_row0=NX, q_len=NC,
                                 segs=((NX, NC),), **att_args)
            at_out = jnp.concatenate([at_out, at_c], axis=0)

        m = merge_branches(rw_out, rt_out, at_out, w_ro_bf, w_to_bf, w_ao_bf, p_gate, 0,
                           layer=l, rows=n_rows, tm=tm_rows)
        h1 = matmul_residual(m, w_out_bf, h, mod, 2, layer=l, n_batch=B, seq=S, tm=tm_rows, tn=512)
        n2 = norm_mod(h1, norm2_w, mod, 3, 4, layer=l, rows=n_rows, n_batch=B, seq=S)
        act = ffn_up_conv_gate(n2, w_up_bf, ffn_conv_w, conv_b3, layer=l, n_x=NX, seq=S, n_ctx=NC, tm=tm_rows, tn=256)
        h = matmul_residual(act, w_down_bf, h1, mod, 5, layer=l, n_batch=B, seq=S, tm=tm_rows, tn=512,
                            tk=ffn_dim // 2)
    return h.reshape(B, S, D)
```

```python
import functools

import jax
import jax.numpy as jnp
from jax import lax
from jax.experimental import pallas as pl
from jax.experimental.pallas import tpu as pltpu

F32 = jnp.float32
BF16 = jnp.bfloat16

V7X_VMEM_LIMIT_CAP = 56 * 1024 * 1024
LANE = 128
BF16_SUBLANES = 16

GRID_W = 64
RWKV_HEAD = 64
RWKV_LN_EPS = 64e-5
RWKV_CHUNK = 64
RWKV_CHUNKS_PER_STEP = 4
RET_V_HEAD = 256
RET_CHUNK = 128
RET_NORM_EPS = 1e-6
ATT_HEAD = 128
ATT_GROUP = 4
ROPE_THETA = 10000.0
NORM_EPS = 1e-6
LOG2_E = 1.4426950408889634

_NT = (((1,), (1,)), ((), ()))
_TN = (((0,), (0,)), ((), ()))


def _vmem_limit(tile_bytes):
    return int(min(max(2 * tile_bytes + (8 << 20), 16 << 20), V7X_VMEM_LIMIT_CAP))


def _dot(x, y, dims=(((1,), (0,)), ((), ()))):
    return lax.dot_general(x.astype(BF16), y.astype(BF16), dims, preferred_element_type=F32)


def _split_bf16(x):
    hi = x.astype(BF16)
    return hi, (x - hi.astype(F32)).astype(BF16)


def _dot3(x, y):
    xh, xl = _split_bf16(x)
    yh, yl = _split_bf16(y)
    return jnp.dot(jnp.concatenate([xh, xh, xl], axis=1), jnp.concatenate([yh, yl, yh], axis=0),
                   preferred_element_type=F32)


def _mm_kernel(a_ref, b_ref, o_ref, *, nk):
    part = _dot(a_ref[...], b_ref[...])
    if nk == 1:
        o_ref[...] = part.astype(o_ref.dtype)
    else:
        k = pl.program_id(2)

        @pl.when(k == 0)
        def _():
            o_ref[...] = part

        @pl.when(k > 0)
        def _():
            o_ref[...] += part


def _pick(n, prefs):
    for p in prefs:
        if n % p == 0:
            return p
    return n


def matmul(a, b, *, layer=None, rows=None, tm=None, tn=None, tk=None, out_dtype=F32):
    M, K = a.shape
    M = rows or M
    N = b.shape[-1]
    assert b.shape[-2] == K
    tm = tm or _pick(M, (1088, 1024, 512, 256, 128, 8))
    tn = tn or _pick(N, (1024, 640, 512, 256, 128))
    tk = tk or K
    assert M % tm == 0 and N % tn == 0 and K % tk == 0, (M, N, K, tm, tn, tk)
    nk = K // tk
    if nk > 1:
        assert out_dtype == F32
    if b.ndim == 3:
        b_spec = pl.BlockSpec((None, tk, tn), lambda i, j, k: (layer, k, j))
    else:
        b_spec = pl.BlockSpec((tk, tn), lambda i, j, k: (k, j))
    tile_bytes = (tm * tk * a.dtype.itemsize + tk * tn * (b.dtype.itemsize + 2)
                  + tm * tn * (jnp.dtype(out_dtype).itemsize + 4))
    return pl.pallas_call(
        functools.partial(_mm_kernel, nk=nk),
        grid=(M // tm, N // tn, nk),
        in_specs=[pl.BlockSpec((tm, tk), lambda i, j, k: (i, k)), b_spec],
        out_specs=pl.BlockSpec((tm, tn), lambda i, j, k: (i, j)),
        out_shape=jax.ShapeDtypeStruct((M, N), out_dtype),
        compiler_params=pltpu.CompilerParams(
            dimension_semantics=("parallel", "parallel", "arbitrary"),
            vmem_limit_bytes=_vmem_limit(tile_bytes)),
        name="matmul",
    )(a, b)


def _row_ids(row0, tm):
    return row0 + lax.broadcasted_iota(jnp.int32, (tm, 1), 0)


def _seq_edges(g, n_x, seq, n_ctx):
    in_x = g < n_x
    starts = jnp.where(in_x, g & (seq - 1), (g - n_x) & (n_ctx - 1)) == 0
    ends = jnp.where(in_x, (g + 1) & (seq - 1), (g + 1 - n_x) & (n_ctx - 1)) == 0
    return starts, ends


def _row_select(table, g, n_batch, seq):
    out = table[n_batch:n_batch + 1]
    seq_id = g >> (seq.bit_length() - 1)
    for b in range(n_batch):
        out = jnp.where(seq_id == b, table[b:b + 1], out)
    return out


def _head_sum(x, ones_bd):
    hi, lo = _split_bf16(x)
    return (jnp.dot(hi, ones_bd, preferred_element_type=F32)
            + jnp.dot(lo, ones_bd, preferred_element_type=F32))


def _mm_res_kernel(a_ref, b_ref, h_ref, g_ref, o_ref, *, nk, tm, n_batch, seq):
    part = _dot(a_ref[...], b_ref[...])
    k = pl.program_id(2)
    row0 = pl.program_id(0) * tm

    def finish(acc):
        gate = _row_select(g_ref[...], _row_ids(row0, tm), n_batch, seq)
        o_ref[...] = h_ref[...] + gate * acc

    if nk == 1:
        finish(part)
    else:
        @pl.when(k == 0)
        def _():
            o_ref[...] = part

        @pl.when((k > 0) & (k < nk - 1))
        def _():
            o_ref[...] += part

        @pl.when(k == nk - 1)
        def _():
            finish(o_ref[...] + part)


def matmul_residual(a, b, h, mod, chunk, *, layer, n_batch, seq, tm, tn, tk=None):
    M, K = a.shape
    N = b.shape[-1]
    tk = tk or K
    assert M % tm == 0 and N % tn == 0 and K % tk == 0
    nk = K // tk
    nn = N // tn
    tile_bytes = tm * tk * 2 + tk * tn * 2 + 3 * tm * tn * 4
    return pl.pallas_call(
        functools.partial(_mm_res_kernel, nk=nk, tm=tm, n_batch=n_batch, seq=seq),
        grid=(M // tm, nn, nk),
        in_specs=[pl.BlockSpec((tm, tk), lambda i, j, k: (i, k)),
                  pl.BlockSpec((None, tk, tn), lambda i, j, k: (layer, k, j)),
                  pl.BlockSpec((tm, tn), lambda i, j, k: (i, j)),
                  pl.BlockSpec((8, tn), lambda i, j, k: (0, chunk * nn + j))],
        out_specs=pl.BlockSpec((tm, tn), lambda i, j, k: (i, j)),
        out_shape=jax.ShapeDtypeStruct((M, N), F32),
        compiler_params=pltpu.CompilerParams(
            dimension_semantics=("parallel", "parallel", "arbitrary"),
            vmem_limit_bytes=_vmem_limit(tile_bytes)),
        name="matmul_residual",
    )(a, b, h, mod)


def _norm_mod_kernel(h_ref, w_ref, sc_ref, sh_ref, o_ref):
    x = h_ref[...]
    gain = w_ref[...] * (1.0 + sc_ref[...])
    y = x * lax.rsqrt(jnp.mean(x * x, axis=-1, keepdims=True) + NORM_EPS) * gain + sh_ref[...]
    o_ref[...] = y.astype(o_ref.dtype)


def norm_mod(h, norm_w, mod, shift_chunk, scale_chunk, *, layer, rows, n_batch, seq, tm=512):
    D = h.shape[1]
    assert rows % tm == 0 and seq % tm == 0

    def table_row(i):
        return jnp.minimum(i * tm // seq, n_batch)

    return pl.pallas_call(
        _norm_mod_kernel,
        grid=(rows // tm,),
        in_specs=[pl.BlockSpec((tm, D), lambda i: (i, 0)),
                  pl.BlockSpec((None, 1, D), lambda i: (layer, 0, 0)),
                  pl.BlockSpec((None, 1, D), lambda i: (table_row(i), 0, scale_chunk)),
                  pl.BlockSpec((None, 1, D), lambda i: (table_row(i), 0, shift_chunk))],
        out_specs=pl.BlockSpec((tm, D), lambda i: (i, 0)),
        out_shape=jax.ShapeDtypeStruct((rows, D), BF16),
        compiler_params=pltpu.CompilerParams(
            dimension_semantics=("parallel",), vmem_limit_bytes=_vmem_limit(tm * D * 10)),
        name="norm_mod",
    )(h, norm_w[:, None, :], mod[:, None, :], mod[:, None, :])


def _ffn_up_kernel(a_ref, ap_ref, an_ref, wv_ref, wg_ref, cv_ref, cg_ref, bv_ref, bg_ref, o_ref, ext_ref,
                   *, tm, halo, n_x, seq, n_ctx, tiles_per_seq):
    i = pl.program_id(0)
    j = pl.program_id(1)

    @pl.when(j == 0)
    def _():
        prev_rows = ap_ref[...]
        next_rows = an_ref[...]
        if tiles_per_seq:
            prev_rows = jnp.where(i % tiles_per_seq == 0, jnp.zeros_like(prev_rows), prev_rows)
            next_rows = jnp.where((i + 1) % tiles_per_seq == 0, jnp.zeros_like(next_rows), next_rows)
        ext_ref[0:halo, :] = prev_rows
        ext_ref[halo:halo + tm, :] = a_ref[...]
        ext_ref[halo + tm:, :] = next_rows

    a_ext = ext_ref[...]

    def conv(w_ref, c_ref, b_ref):
        u = jnp.dot(a_ext, w_ref[...].astype(BF16), preferred_element_type=F32)
        prev = u[halo - 1:halo - 1 + tm]
        nxt = u[halo + 1:halo + 1 + tm]
        if not tiles_per_seq:
            starts, ends = _seq_edges(_row_ids(i * tm, tm), n_x, seq, n_ctx)
            prev = jnp.where(starts, 0.0, prev)
            nxt = jnp.where(ends, 0.0, nxt)
        c = c_ref[...]
        return prev * c[0:1] + u[halo:halo + tm] * c[1:2] + nxt * c[2:3] + b_ref[...]

    val = conv(wv_ref, cv_ref, bv_ref)
    gate = conv(wg_ref, cg_ref, bg_ref)
    o_ref[...] = (jax.nn.silu(gate) * val).astype(o_ref.dtype)


def ffn_up_conv_gate(a, w_up, conv_w, conv_b, *, layer, n_x, seq, n_ctx, tm, tn=512):
    R, D = a.shape
    F = w_up.shape[-1] // 2
    halo = BF16_SUBLANES
    assert R % tm == 0 and tm % halo == 0 and F % tn == 0
    assert seq & (seq - 1) == 0 and n_ctx & (n_ctx - 1) == 0
    nf = F // tn
    nh = tm // halo
    last = R // halo - 1
    tile_bytes = ((tm * D * 2) // 2 + 2 * D * tn * (w_up.dtype.itemsize + 1) + tm * tn * 2
                  + ((tm + 2 * halo) * D * 2) // 2 + 4 * tm * tn * 4)
    tiles_per_seq = seq // tm if (R == n_x and seq % tm == 0) else 0
    kern = functools.partial(_ffn_up_kernel, tm=tm, halo=halo, n_x=n_x, seq=seq, n_ctx=n_ctx,
                             tiles_per_seq=tiles_per_seq)
    return pl.pallas_call(
        kern,
        grid=(R // tm, nf),
        in_specs=[pl.BlockSpec((tm, D), lambda i, j: (i, 0), pipeline_mode=pl.Buffered(1)),
                  pl.BlockSpec((halo, D), lambda i, j: (jnp.maximum(i * nh - 1, 0), 0)),
                  pl.BlockSpec((halo, D), lambda i, j: (jnp.minimum((i + 1) * nh, last), 0)),
                  pl.BlockSpec((None, D, tn), lambda i, j: (layer, 0, j)),
                  pl.BlockSpec((None, D, tn), lambda i, j: (layer, 0, nf + j)),
                  pl.BlockSpec((None, 3, tn), lambda i, j: (layer, 0, j)),
                  pl.BlockSpec((None, 3, tn), lambda i, j: (layer, 0, nf + j)),
                  pl.BlockSpec((None, 1, tn), lambda i, j: (layer, 0, j)),
                  pl.BlockSpec((None, 1, tn), lambda i, j: (layer, 0, nf + j))],
        out_specs=pl.BlockSpec((tm, tn), lambda i, j: (i, j)),
        out_shape=jax.ShapeDtypeStruct((R, F), BF16),
        scratch_shapes=[pltpu.VMEM((tm + 2 * halo, D), BF16)],
        compiler_params=pltpu.CompilerParams(
            dimension_semantics=("parallel", "arbitrary"),
            vmem_limit_bytes=_vmem_limit(tile_bytes)),
        name="ffn_up_conv_gate",
    )(a, a, a, w_up, w_up, conv_w, conv_w, conv_b, conv_b)


def _merge_kernel(rw_ref, rt_ref, at_ref, w1_ref, w2_ref, w3_ref, g1_ref, g2_ref, g3_ref, o_ref):
    def br(x_ref, w_ref, g_ref):
        return jax.nn.sigmoid(g_ref[...]) * _dot(x_ref[...], w_ref[...])

    m = br(rw_ref, w1_ref, g1_ref) + br(rt_ref, w2_ref, g2_ref) + br(at_ref, w3_ref, g3_ref)
    o_ref[...] = m.astype(o_ref.dtype)


def merge_branches(rw, rt, at, w1, w2, w3, p_all, gate_col0, *, layer, rows, tm, tn=512):
    D = w1.shape[-1]
    assert rows % tm == 0 and D % tn == 0 and gate_col0 % tn == 0
    g0 = gate_col0 // tn
    nd = D // tn
    k1, k2, k3 = rw.shape[1], rt.shape[1], at.shape[1]
    ks = k1 + k2 + k3
    tile_bytes = tm * ks * 2 + ks * tn * 2 + 3 * tm * tn * 4 + tm * tn * 2 + 3 * tm * tn * 4

    def gspec(i):
        return pl.BlockSpec((tm, tn), lambda m, n, i=i: (m, g0 + i * nd + n))

    def wspec(k):
        return pl.BlockSpec((None, k, tn), lambda m, n: (layer, 0, n))

    return pl.pallas_call(
        _merge_kernel,
        grid=(rows // tm, nd),
        in_specs=[pl.BlockSpec((tm, k1), lambda m, n: (m, 0)),
                  pl.BlockSpec((tm, k2), lambda m, n: (m, 0)),
                  pl.BlockSpec((tm, k3), lambda m, n: (m, 0)),
                  wspec(k1), wspec(k2), wspec(k3),
                  gspec(0), gspec(1), gspec(2)],
        out_specs=pl.BlockSpec((tm, tn), lambda m, n: (m, n)),
        out_shape=jax.ShapeDtypeStruct((rows, D), BF16),
        compiler_params=pltpu.CompilerParams(
            dimension_semantics=("parallel", "parallel"),
            vmem_limit_bytes=_vmem_limit(tile_bytes)),
        name="merge_branches",
    )(rw, rt, at, w1, w2, w3, p_all, p_all, p_all)


def _softplus(z):
    return jnp.maximum(z, 0.0) + jnp.log(1.0 + jnp.exp(-jnp.abs(z)))


def _rwkv_prep_kernel(p_ref, pp_ref, pn_ref, mup_ref, mun_ref, vec_ref, bias_ref, lora_ref, ones_ref,
                      r_ref, v_ref, kk_ref, lw_ref, kd_ref, as_ref, bonus_ref, gate_ref,
                      *, tm, dim, lora_w, lora_a, n_x, seq, n_ctx):
    halo = pp_ref.shape[0]
    g = _row_ids(pl.program_id(0) * tm, tm)
    starts, ends = _seq_edges(g, n_x, seq, n_ctx)
    p = p_ref[...]
    ext = jnp.concatenate([pp_ref[...], p, pn_ref[...]], axis=0)
    prev = jnp.where(starts, 0.0, ext[halo - 1:halo - 1 + tm])
    nxt = jnp.where(ends, 0.0, ext[halo + 1:halo + 1 + tm])
    x = p + mup_ref[...] * (prev - p) + mun_ref[...] * (nxt - p)

    r = x[:, :dim]
    k = x[:, dim:2 * dim]
    v = x[:, 2 * dim:3 * dim]
    tail = x[:, 3 * dim:]
    lane = lax.broadcasted_iota(jnp.int32, tail.shape, 1)
    act = jnp.where(lane < lora_w, jnp.tanh(tail), jnp.where(lane < lora_w + lora_a, tail, jax.nn.sigmoid(tail)))
    pre = _dot(act, lora_ref[...]) + bias_ref[...]

    vec = vec_ref[...]
    ones_bd = ones_ref[...]
    kk = k * vec[0:1]
    kk = kk * lax.rsqrt(jnp.maximum(_head_sum(kk * kk, ones_bd), 1e-24))
    r_ref[...] = r
    v_ref[...] = v
    kk_ref[...] = kk
    bonus = jnp.zeros_like(r)
    for d in range(2):
        w_pre = pre[:, 2 * d * dim:(2 * d + 1) * dim]
        a_pre = pre[:, (2 * d + 1) * dim:(2 * d + 2) * dim]
        lw_ref[d] = -jnp.exp(-_softplus(-w_pre) - 0.5)
        a_sig = jax.nn.sigmoid(a_pre)
        kd = k * (1.0 + (a_sig - 1.0) * vec[1:2])
        as_ref[d] = a_sig
        kd_ref[d] = kd
        bonus = bonus + _head_sum(r * kd * vec[2:3], ones_bd)
    bonus_ref[...] = bonus * v
    gate_ref[...] = pre[:, 4 * dim:]


def rwkv_prep(p_all, mu_prev, mu_next, vecs, bias, lora, ones_bd, *, layer, width, dim, lora_w, lora_a,
              n_x, seq, n_ctx, tm=256):
    R = p_all.shape[0]
    halo = 8
    assert R % tm == 0 and tm % halo == 0
    nh = tm // halo
    last = R // halo - 1
    row = pl.BlockSpec((tm, dim), lambda i: (i, 0))
    per_dir = pl.BlockSpec((2, tm, dim), lambda i: (0, i, 0))

    def par(a):
        return pl.BlockSpec((None,) + a.shape[1:], lambda i: (layer,) + (0,) * (a.ndim - 1))

    f = jax.ShapeDtypeStruct((R, dim), F32)
    f2 = jax.ShapeDtypeStruct((2, R, dim), F32)
    kern = functools.partial(_rwkv_prep_kernel, tm=tm, dim=dim, lora_w=lora_w, lora_a=lora_a,
                             n_x=n_x, seq=seq, n_ctx=n_ctx)
    return pl.pallas_call(
        kern,
        grid=(R // tm,),
        in_specs=[pl.BlockSpec((tm, width), lambda i: (i, 0)),
                  pl.BlockSpec((halo, width), lambda i: (jnp.maximum(i * nh - 1, 0), 0)),
                  pl.BlockSpec((halo, width), lambda i: (jnp.minimum((i + 1) * nh, last), 0)),
                  par(mu_prev), par(mu_next), par(vecs), par(bias), par(lora),
                  pl.BlockSpec(ones_bd.shape, lambda i: (0, 0))],
        out_specs=[row, row, row, per_dir, per_dir, per_dir, row, row],
        out_shape=[f, f, f, f2, f2, f2, f, f],
        compiler_params=pltpu.CompilerParams(
            dimension_semantics=("parallel",),
            vmem_limit_bytes=_vmem_limit(tm * (width + 20 * dim) * 4 + lora.shape[1] * lora.shape[2] * 2
                                         + dim * dim * 2)),
        name="rwkv_prep",
    )(p_all, p_all, p_all, mu_prev, mu_next, vecs, bias, lora, ones_bd)


def _rwkv_finish_kernel(y_ref, bonus_ref, gate_ref, ln_ref, ones_ref, o_ref, *, head):
    ones_bd = ones_ref[...]
    y = y_ref[0] + y_ref[1]
    yc = y - _head_sum(y, ones_bd) * (1.0 / head)
    var = _head_sum(yc * yc, ones_bd) * (1.0 / head)
    ln = ln_ref[...]
    out = yc * lax.rsqrt(var + RWKV_LN_EPS) * ln[0:1] + ln[1:2] + bonus_ref[...]
    o_ref[...] = (out * gate_ref[...]).astype(o_ref.dtype)


def rwkv_finish(y2, bonus, gate, ln_wb, ones_bd, *, layer, tm=256):
    _, R, dim = y2.shape
    assert R % tm == 0
    row = pl.BlockSpec((tm, dim), lambda i: (i, 0))
    return pl.pallas_call(
        functools.partial(_rwkv_finish_kernel, head=RWKV_HEAD),
        grid=(R // tm,),
        in_specs=[pl.BlockSpec((2, tm, dim), lambda i: (0, i, 0)), row, row,
                  pl.BlockSpec((None, 2, dim), lambda i: (layer, 0, 0)),
                  pl.BlockSpec(ones_bd.shape, lambda i: (0, 0))],
        out_specs=row,
        out_shape=jax.ShapeDtypeStruct((R, dim), BF16),
        compiler_params=pltpu.CompilerParams(
            dimension_semantics=("parallel",), vmem_limit_bytes=_vmem_limit(tm * dim * 40 + dim * dim * 2)),
        name="rwkv_finish",
    )(y2, bonus, gate, ln_wb, ones_bd)


def _row_block_fn(n_batch, seq, n_ctx, chunk):
    nctx = n_ctx // chunk
    nx = seq // chunk
    coff = n_batch * nx

    def rb(b, d, c):
        fwd = jnp.where(c < nctx, coff + b * nctx + c, b * nx + (c - nctx))
        bwd = jnp.where(c < nctx, coff + b * nctx + (nctx - 1 - c), b * nx + (nctx + nx - 1 - c))
        return jnp.where(d == 0, fwd, bwd)

    return rb, nctx + nx


def _rwkv_kernel(r_ref, v_ref, kk_ref, lw_ref, kd_ref, as_ref, y_ref, s_ref, *, C, npair, nsub):
    fwd = pl.program_id(1) == 0

    @pl.when(pl.program_id(2) == 0)
    def _():
        s_ref[...] = jnp.zeros_like(s_ref)

    W = 2 * C
    row = lax.broadcasted_iota(jnp.int32, (C, W), 0)
    lane = lax.broadcasted_iota(jnp.int32, (C, W), 1)
    col = lane & (C - 1)
    ahead = jnp.where(fwd, row - col, col - row)
    strict = ahead > 0
    lower = ahead >= 0
    left = lane < C
    eye = jnp.where(ahead == 0, 1.0, 0.0)
    r0 = lax.broadcasted_iota(jnp.int32, (C, C), 0)
    c0 = lax.broadcasted_iota(jnp.int32, (C, C), 1)
    tri = jnp.where(jnp.where(fwd, r0 - c0, c0 - r0) >= 0, 1.0, 0.0).astype(BF16)
    bd_mask = (lax.broadcasted_iota(jnp.int32, (W, W), 0) < C) == (lax.broadcasted_iota(jnp.int32, (W, W), 1) < C)

    def only(x, first):
        return jnp.where(left, x, 0.0) if first else jnp.where(left, 0.0, x)

    def stack(*xs):
        return jnp.concatenate(xs, axis=0)

    def bdiag(p):
        return stack(only(p, True), only(p, False))

    items = [(q, p) for q in range(nsub) for p in range(npair)]
    P = range(len(items))
    rows = [pl.ds(pl.multiple_of(jnp.where(fwd, q * C, (nsub - 1 - q) * C), C), C) for q in range(nsub)]
    at = [(rows[q], slice(p * W, (p + 1) * W)) for q, p in items]
    r = [r_ref[ix] for ix in at]
    v = [v_ref[ix] for ix in at]
    kk = [kk_ref[ix] for ix in at]
    lw = [lw_ref[ix] for ix in at]
    kd = [kd_ref[ix] for ix in at]
    b = [kk[p] * as_ref[at[p]] for p in P]

    def cumsum(x):
        l1 = x.astype(BF16)
        e1 = x - l1.astype(F32)
        l2 = e1.astype(BF16)
        l3 = (e1 - l2.astype(F32)).astype(BF16)
        g3 = _dot(tri, jnp.concatenate([l1, l2, l3], axis=1))
        return g3[:, :W] + g3[:, W:2 * W] + g3[:, 2 * W:]

    g_in = [cumsum(lw[p]) for p in P]
    g_last = [jnp.where(fwd, g[C - 1:C, :], g[0:1, :]) for g in g_in]
    e_neg = [jnp.exp(-g) for g in g_in]
    a_t = [-kk[p] * jnp.exp(g_in[p] - lw[p]) for p in P]
    r_t = [r[p] * jnp.exp(g_in[p]) for p in P]
    b_t = [b[p] * e_neg[p] for p in P]
    k_t = [kd[p] * e_neg[p] for p in P]
    e_rem = [jnp.exp(g_last[p] - g_in[p]) for p in P]
    b_h = [b[p] * e_rem[p] for p in P]
    k_h = [kd[p] * e_rem[p] for p in P]

    res_a = [_dot(stack(only(a_t[p], True), only(r_t[p], True)), stack(k_t[p], b_t[p]), _NT) for p in P]
    res_b = [_dot(stack(only(a_t[p], False), only(r_t[p], False)), stack(b_t[p], k_t[p]), _NT) for p in P]
    ak = [jnp.where(strict, jnp.where(left, res_a[p][:C], res_b[p][:C]), 0.0) for p in P]
    ab = [jnp.where(strict, jnp.where(left, res_b[p][:C], res_a[p][:C]), 0.0) for p in P]
    rk = [jnp.where(lower, jnp.where(left, res_a[p][C:], res_b[p][C:]), 0.0) for p in P]
    rb = [jnp.where(lower, jnp.where(left, res_b[p][C:], res_a[p][C:]), 0.0) for p in P]

    inv = [eye + ab[p] for p in P]
    pw = [_dot3(ab[p], bdiag(ab[p])) for p in P]
    n = 2
    while n < C:
        both = [_dot3(stack(inv[p], pw[p]), bdiag(pw[p])) for p in P]
        inv = [inv[p] + both[p][:C] for p in P]
        pw = [both[p][C:] for p in P]
        n *= 2

    s = [s_ref[h] for h in range(npair)]
    for q in range(nsub):
        Q = range(q * npair, (q + 1) * npair)
        a_s = {p: _dot(stack(a_t[p], r_t[p]), s[p - Q[0]], _NT) for p in Q}
        inner = {p: a_s[p][:C] + _dot(ak[p], stack(only(v[p], True), only(v[p], False))) for p in Q}
        u = {p: _dot(inv[p], stack(only(inner[p], False), only(inner[p], True))) for p in Q}
        for p in Q:
            y_ref[at[p]] = a_s[p][C:] + _dot(
                jnp.concatenate([rb[p], rk[p]], axis=1),
                stack(only(u[p], False), only(u[p], True), only(v[p], True), only(v[p], False)))
        for p in Q:
            upd = _dot(stack(u[p], v[p]), stack(b_h[p], k_h[p]), _TN)
            s[p - Q[0]] = s[p - Q[0]] * jnp.exp(g_last[p]) + jnp.where(bd_mask, upd, 0.0)
    for h in range(npair):
        s_ref[h] = s[h]


def rwkv_scan(r, v, kk, lw, kd, a_sig, *, n_batch, seq, n_ctx, C=RWKV_CHUNK):
    R, HN = r.shape
    npair = HN // (2 * RWKV_HEAD)
    nsub = RWKV_CHUNKS_PER_STEP
    rb, steps = _row_block_fn(n_batch, seq, n_ctx, nsub * C)
    shared = pl.BlockSpec((nsub * C, HN), lambda b, d, c: (rb(b, d, c), 0))
    per_dir = pl.BlockSpec((None, nsub * C, HN), lambda b, d, c: (d, rb(b, d, c), 0))
    return pl.pallas_call(
        functools.partial(_rwkv_kernel, C=C, npair=npair, nsub=nsub),
        grid=(n_batch, 2, steps),
        in_specs=[shared, shared, shared, per_dir, per_dir, per_dir],
        out_specs=per_dir,
        out_shape=jax.ShapeDtypeStruct((2, R, HN), F32),
        scratch_shapes=[pltpu.VMEM((npair, 2 * RWKV_HEAD, 2 * RWKV_HEAD), F32)],
        compiler_params=pltpu.CompilerParams(dimension_semantics=("parallel", "parallel", "arbitrary")),
        name="rwkv_scan",
    )(r, v, kk, lw, kd, a_sig)


def _ret_kernel(lg_ref, q_ref, k_ref, v_ref, cos_ref, sin_ref, o_ref, s_ref, *, C, H, dk, dv):
    fwd = pl.program_id(1) == 0

    @pl.when(pl.program_id(2) == 0)
    def _():
        s_ref[...] = jnp.zeros_like(s_ref)

    cos = cos_ref[...]
    sin = sin_ref[...]
    row = lax.broadcasted_iota(jnp.int32, (C, C), 0)
    col = lax.broadcasted_iota(jnp.int32, (C, C), 1)
    ahead = jnp.where(fwd, row - col, col - row)
    dist = jnp.maximum(ahead, 0).astype(F32)
    idx = lax.broadcasted_iota(jnp.int32, (C, 1), 0)
    pos = jnp.where(fwd, idx, C - 1 - idx).astype(F32)

    hs = range(H)
    lg = [lg_ref[h][:, 0:1] for h in hs]
    q = [q_ref[:, h * dk:(h + 1) * dk] for h in hs]
    k = [k_ref[:, h * dk:(h + 1) * dk] * (dk ** -0.5) for h in hs]
    q = [x * cos + pltpu.roll(x, dk // 2, 1) * sin for x in q]
    k = [x * cos + pltpu.roll(x, dk // 2, 1) * sin for x in k]
    v = [v_ref[:, h * dv:(h + 1) * dv] for h in hs]
    s = [s_ref[h] for h in hs]
    scores = [_dot(q[h], k[h], _NT) * jnp.where(ahead >= 0, jnp.exp(dist * lg[h]), 0.0) for h in hs]
    carry = [_dot(q[h] * jnp.exp((pos + 1.0) * lg[h]), s[h]) for h in hs]
    for h in hs:
        o_ref[:, h * dv:(h + 1) * dv] = _dot(scores[h], v[h]) + carry[h]
    for h in hs:
        s_ref[h] = s[h] * jnp.exp(C * lg[h]) + _dot(k[h] * jnp.exp((C - 1.0 - pos) * lg[h]), v[h], _TN)


def retention_scan(p_ret, H, cos2, sin2, lg, *, n_batch, seq, n_ctx, C=RET_CHUNK):
    R = p_ret.shape[0]
    dv = RET_V_HEAD
    dk = dv // 2
    rb, steps = _row_block_fn(n_batch, seq, n_ctx, C)
    lg_b = jnp.broadcast_to(lg.astype(F32).reshape(2, H, 1, 1), (2, H, 1, LANE))
    tab = pl.BlockSpec((C, dk), lambda b, d, c: (rb(b, d, c), 0))
    return pl.pallas_call(
        functools.partial(_ret_kernel, C=C, H=H, dk=dk, dv=dv),
        grid=(n_batch, 2, steps),
        in_specs=[pl.BlockSpec((None, H, 1, LANE), lambda b, d, c: (d, 0, 0, 0)),
                  pl.BlockSpec((C, H * dk), lambda b, d, c: (rb(b, d, c), 0)),
                  pl.BlockSpec((C, H * dk), lambda b, d, c: (rb(b, d, c), 1)),
                  pl.BlockSpec((C, H * dv), lambda b, d, c: (rb(b, d, c), 1)),
                  tab, tab],
        out_specs=pl.BlockSpec((None, C, H * dv), lambda b, d, c: (d, rb(b, d, c), 0)),
        out_shape=jax.ShapeDtypeStruct((2, R, H * dv), F32),
        scratch_shapes=[pltpu.VMEM((H, dk, dv), F32)],
        compiler_params=pltpu.CompilerParams(dimension_semantics=("parallel", "parallel", "arbitrary")),
        name="retention_scan",
    )(lg_b, p_ret, p_ret, p_ret, cos2, sin2)


def _ret_finish_kernel(o_ref, g_ref, out_ref):
    o = o_ref[0] + o_ref[1]
    oc = o - jnp.mean(o, axis=-1, keepdims=True)
    y = oc * lax.rsqrt(jnp.mean(oc * oc, axis=-1, keepdims=True) + RET_NORM_EPS)
    out_ref[...] = (y * jax.nn.silu(g_ref[...])).astype(out_ref.dtype)


def retention_finish(o2, p_all, gcol0, *, tm=512):
    _, R, HD = o2.shape
    dv = RET_V_HEAD
    assert R % tm == 0 and gcol0 % dv == 0
    g0 = gcol0 // dv
    return pl.pallas_call(
        _ret_finish_kernel,
        grid=(R // tm, HD // dv),
        in_specs=[pl.BlockSpec((2, tm, dv), lambda i, h: (0, i, h)),
                  pl.BlockSpec((tm, dv), lambda i, h: (i, g0 + h))],
        out_specs=pl.BlockSpec((tm, dv), lambda i, h: (i, h)),
        out_shape=jax.ShapeDtypeStruct((R, HD), BF16),
        compiler_params=pltpu.CompilerParams(dimension_semantics=("parallel", "parallel")),
        name="retention_finish",
    )(o2, p_all)


def _norm_rope(x, w, cos, sin):
    quarter = x.shape[1] // 4
    y = x * lax.rsqrt(jnp.mean(x * x, axis=-1, keepdims=True) + NORM_EPS) * w
    lane = lax.broadcasted_iota(jnp.int32, y.shape, 1)
    first = (lane & (2 * quarter - 1)) < quarter
    swapped = jnp.where(first, pltpu.roll(y, 3 * quarter, 1), pltpu.roll(y, quarter, 1))
    return y * cos + swapped * sin


def _kv_prep_kernel(k_ref, v_ref, w_ref, cos_ref, sin_ref, ko_ref, vo_ref, *, dh):
    cos = cos_ref[...]
    sin = sin_ref[...]
    w = w_ref[...]
    for h in range(k_ref.shape[1] // dh):
        sl = slice(h * dh, (h + 1) * dh)
        ko_ref[:, sl] = _norm_rope(k_ref[:, sl], w, cos, sin).astype(ko_ref.dtype)
        vo_ref[sl, :] = v_ref[:, sl].T.astype(vo_ref.dtype)


def attention_kv_prep(p_att, kcol0, width, k_norm, cos, sin, *, layer, tm=512, dh=ATT_HEAD):
    R = p_att.shape[0]
    assert R % tm == 0 and kcol0 % width == 0
    kb = kcol0 // width
    tab = pl.BlockSpec((tm, dh), lambda i: (i, 0))
    return pl.pallas_call(
        functools.partial(_kv_prep_kernel, dh=dh),
        grid=(R // tm,),
        in_specs=[pl.BlockSpec((tm, width), lambda i: (i, kb)),
                  pl.BlockSpec((tm, width), lambda i: (i, kb + 1)),
                  pl.BlockSpec((None, 1, dh), lambda i: (layer, 0, 0)), tab, tab],
        out_specs=[pl.BlockSpec((tm, width), lambda i: (i, 0)),
                   pl.BlockSpec((width, tm), lambda i: (0, i))],
        out_shape=[jax.ShapeDtypeStruct((R, width), BF16), jax.ShapeDtypeStruct((width, R), BF16)],
        compiler_params=pltpu.CompilerParams(dimension_semantics=("parallel",)),
        name="attention_kv_prep",
    )(p_att, p_att, k_norm[:, None, :], cos, sin)


def _attn_kernel(q_ref, w_ref, cos_ref, sin_ref, *refs, nseg, group, dh):
    k_refs, vt_refs, o_ref = refs[:nseg], refs[nseg:2 * nseg], refs[2 * nseg]
    ks = [r[...] for r in k_refs]
    vts = [r[...] for r in vt_refs]
    cos = cos_ref[...]
    sin = sin_ref[...]
    w = w_ref[...] * (dh ** -0.5 * LOG2_E)

    def scores(g):
        q = _norm_rope(q_ref[:, g * dh:(g + 1) * dh], w, cos, sin).astype(BF16)
        return [lax.dot_general(k, q, _NT, preferred_element_type=F32) for k in ks]

    def softmax(ss):
        m = functools.reduce(jnp.maximum, [jnp.max(s, axis=0, keepdims=True) for s in ss])
        ps = [jnp.exp2(s - m) for s in ss]
        l = functools.reduce(jnp.add, [jnp.sum(p, axis=0, keepdims=True) for p in ps])
        return [p.astype(BF16) for p in ps], l

    def output(g, ps, l):
        ot = functools.reduce(jnp.add, [jnp.dot(vt, p, preferred_element_type=F32)
                                        for p, vt in zip(ps, vts)])
        o_ref[:, g * dh:(g + 1) * dh] = (ot / l).T.astype(o_ref.dtype)

    ss, pl_ = {}, {}
    for step in range(group + 2):
        if step < group:
            ss[step] = scores(step)
        if 0 <= step - 1 < group:
            pl_[step - 1] = softmax(ss.pop(step - 1))
        if 0 <= step - 2 < group:
            output(step - 2, *pl_.pop(step - 2))


def gqa_attention(p_att, q_norm, cos, sin, k, vt, *, layer, n_batch, q_row0, q_len, segs, tq=256,
                  group=ATT_GROUP, dh=ATT_HEAD):
    hkv = k.shape[1] // dh
    gw = group * dh
    assert q_len % tq == 0 and q_row0 % tq == 0
    nq = q_len // tq
    q0 = q_row0 // tq
    for row0, ln in segs:
        assert row0 % ln == 0
    k_specs = [pl.BlockSpec((ln, dh), lambda b, h, i, row0=row0, ln=ln: (row0 // ln + b, h))
               for row0, ln in segs]
    vt_specs = [pl.BlockSpec((dh, ln), lambda b, h, i, row0=row0, ln=ln: (h, row0 // ln + b))
                for row0, ln in segs]
    tk = sum(ln for _, ln in segs)
    tile_bytes = tq * gw * 6 + 2 * tk * dh * 2 + 3 * tq * tk * 4
    tab = pl.BlockSpec((tq, dh), lambda b, h, i: (q0 + b * nq + i, 0))
    return pl.pallas_call(
        functools.partial(_attn_kernel, nseg=len(segs), group=group, dh=dh),
        grid=(n_batch, hkv, nq),
        in_specs=[pl.BlockSpec((tq, gw), lambda b, h, i: (q0 + b * nq + i, h)),
                  pl.BlockSpec((None, 1, dh), lambda b, h, i: (layer, 0, 0)), tab, tab] + k_specs + vt_specs,
        out_specs=pl.BlockSpec((tq, gw), lambda b, h, i: (b * nq + i, h)),
        out_shape=jax.ShapeDtypeStruct((n_batch * q_len, hkv * gw), BF16),
        compiler_params=pltpu.CompilerParams(
            dimension_semantics=("parallel", "parallel", "parallel"),
            vmem_limit_bytes=_vmem_limit(tile_bytes)),
        name="gqa_attention",
    )(p_att, q_norm[:, None, :], cos, sin, *([k] * len(segs)), *([vt] * len(segs)))


def _rope_angles(pos, dim):
    inv_freq = ROPE_THETA ** (-jnp.arange(0, dim, 2, dtype=F32) / dim)
    return pos.astype(F32)[:, None] * inv_freq[None, :]


def _rope_tables(angle_groups, n_pad_rows, reps):
    cos = jnp.concatenate([jnp.concatenate([jnp.cos(a), jnp.cos(a)], axis=1) for a in angle_groups], axis=1)
    sin = jnp.concatenate([jnp.concatenate([-jnp.sin(a), jnp.sin(a)], axis=1) for a in angle_groups], axis=1)
    w = cos.shape[1]
    cos = jnp.concatenate([jnp.tile(cos, (reps, 1)), jnp.ones((n_pad_rows, w), F32)], axis=0)
    sin = jnp.concatenate([jnp.tile(sin, (reps, 1)), jnp.zeros((n_pad_rows, w), F32)], axis=0)
    return cos, sin


def kernel(x, c, ctx, c_ctx, w_ada, b_ada, norm1_w, norm2_w, w_in, rwkv_mu_prev, rwkv_mu_next, rwkv_w0, rwkv_w2, rwkv_a0, rwkv_a2, rwkv_g2, rwkv_k_k, rwkv_k_a, rwkv_r_k, rwkv_ln_w, rwkv_ln_b, w_rwkv_out, ret_log_decay, w_ret_out, attn_q_norm, attn_k_norm, w_attn_out, w_out, w_ffn_up, ffn_conv_w, ffn_conv_b, w_ffn_down):
    B, S, D = x.shape
    NC = ctx.shape[1]
    NX = B * S
    R = NX + B * NC
    depth = w_in.shape[0]
    rw_dim = w_rwkv_out.shape[1]
    lora_w = rwkv_w2.shape[2]
    lora_a = rwkv_a2.shape[2]
    rw_proj = rwkv_mu_prev.shape[1]
    ret_dim = w_ret_out.shape[1]
    ret_heads = ret_dim // RET_V_HEAD
    ret_qk = RET_V_HEAD // 2
    nqk = ret_heads * ret_qk
    ret_proj = 2 * nqk + 2 * ret_dim
    att_dim = w_attn_out.shape[1]
    att_heads = att_dim // ATT_HEAD
    att_kv = att_heads // ATT_GROUP
    att_proj = att_dim + 2 * att_kv * ATT_HEAD
    mix_proj = rw_proj + ret_proj + att_proj
    ffn_dim = w_ffn_down.shape[1]

    rw_width = rw_proj + (-rw_proj % 512)
    tail_w = rw_width - 3 * rw_dim
    w_in_rw = jnp.pad(w_in[..., :rw_proj].astype(BF16), ((0, 0), (0, 0), (0, rw_width - rw_proj)))
    w_in_ret = w_in[..., rw_proj:rw_proj + ret_proj].astype(BF16)
    w_in_att = w_in[..., rw_proj + ret_proj:mix_proj].astype(BF16)
    w_in_gate = w_in[..., mix_proj:].astype(BF16)
    w_down_bf = w_ffn_down.astype(BF16)
    w_out_bf = w_out.astype(BF16)
    w_ro_bf, w_to_bf, w_ao_bf = w_rwkv_out.astype(BF16), w_ret_out.astype(BF16), w_attn_out.astype(BF16)
    conv_b3 = ffn_conv_b[:, None, :]

    lora_g = rwkv_g2.shape[1]
    mu_pad = ((0, 0), (0, rw_width - rw_proj))
    mu_prev_p = jnp.pad(rwkv_mu_prev, mu_pad)[:, None, :]
    mu_next_p = jnp.pad(rwkv_mu_next, mu_pad)[:, None, :]
    rw_vecs = jnp.stack([rwkv_k_k, rwkv_k_a, rwkv_r_k.reshape(depth, rw_dim)], axis=1)
    rw_bias = jnp.concatenate([rwkv_w0[:, 0], rwkv_a0[:, 0], rwkv_w0[:, 1], rwkv_a0[:, 1],
                               jnp.zeros((depth, rw_dim), F32)], axis=-1)[:, None, :]
    rw_lora = jnp.zeros((depth, tail_w, 5 * rw_dim), F32)
    for d in range(2):
        rw_lora = rw_lora.at[:, :lora_w, 2 * d * rw_dim:(2 * d + 1) * rw_dim].set(rwkv_w2[:, d])
        rw_lora = rw_lora.at[:, lora_w:lora_w + lora_a, (2 * d + 1) * rw_dim:(2 * d + 2) * rw_dim].set(rwkv_a2[:, d])
    rw_lora = rw_lora.at[:, lora_w + lora_a:lora_w + lora_a + lora_g, 4 * rw_dim:].set(rwkv_g2).astype(BF16)
    rw_ln = jnp.stack([rwkv_ln_w, rwkv_ln_b], axis=1)
    head_id = jnp.arange(rw_dim) // RWKV_HEAD
    ones_bd = (head_id[:, None] == head_id[None, :]).astype(BF16)

    pos = jnp.arange(S)
    att_cos, att_sin = _rope_tables([_rope_angles(pos // GRID_W, ATT_HEAD // 2),
                                     _rope_angles(pos % GRID_W, ATT_HEAD // 2)], B * NC, B)
    ret_cos, ret_sin = _rope_tables([_rope_angles(pos, ret_qk)], B * NC, B)

    silu_all = jnp.zeros((8, D), F32).at[:B].set(jax.nn.silu(c)).at[B].set(jax.nn.silu(c_ctx))

    h = jnp.concatenate([x.reshape(NX, D), ctx.reshape(B * NC, D)], axis=0)
    for l in range(depth):
        ctx_out = l < depth - 1
        n_rows = R if ctx_out else NX
        tm_rows = _pick(n_rows, (1088, 1024, 512, 256))
        mod = matmul(silu_all, w_ada, layer=l, tm=8, tn=1024) + b_ada[l][None, :]
        n1 = norm_mod(h, norm1_w, mod, 0, 1, layer=l, rows=R, n_batch=B, seq=S)
        p_rw = matmul(n1, w_in_rw, layer=l, tn=512)
        p_ret = matmul(n1, w_in_ret, layer=l, tn=_pick(ret_proj, (1024, 768, 512)))
        p_att = matmul(n1, w_in_att, layer=l, tn=_pick(att_proj, (1024, 768, 512)))
        p_gate = matmul(n1, w_in_gate, layer=l, rows=n_rows, tm=tm_rows, tn=1024)

        r_, v_, kk, lw, kd, a_sig, bonus, rw_gate = rwkv_prep(
            p_rw, mu_prev_p, mu_next_p, rw_vecs, rw_bias, rw_lora, ones_bd, layer=l, width=rw_width,
            dim=rw_dim, lora_w=lora_w, lora_a=lora_a, n_x=NX, seq=S, n_ctx=NC)
        y2 = rwkv_scan(r_, v_, kk, lw, kd, a_sig, n_batch=B, seq=S, n_ctx=NC)
        rw_out = rwkv_finish(y2, bonus, rw_gate, rw_ln, ones_bd, layer=l)

        o2 = retention_scan(p_ret, ret_heads, ret_cos, ret_sin, ret_log_decay[l], n_batch=B, seq=S, n_ctx=NC)
        rt_out = retention_finish(o2, p_ret, 2 * nqk + ret_dim)

        ka, vta = attention_kv_prep(p_att, att_dim, att_kv * ATT_HEAD, attn_k_norm, att_cos, att_sin, layer=l)
        att_args = dict(layer=l, n_batch=B)
        at_out = gqa_attention(p_att, attn_q_norm, att_cos, att_sin, ka, vta, q_row0=0, q_len=S,
                               segs=((NX, NC), (0, S)), tq=512, **att_args)
        if ctx_out:
            at_c = gqa_attention(p_att, attn_q_norm, att_cos, att_sin, ka, vta, q_row0=NX, q_len=NC,
                                 segs=((NX, NC),), **att_args)
            at_out = jnp.concatenate([at_out, at_c], axis=0)

        m = merge_branches(rw_out, rt_out, at_out, w_ro_bf, w_to_bf, w_ao_bf, p_gate, 0,
                           layer=l, rows=n_rows, tm=tm_rows)
        h1 = matmul_residual(m, w_out_bf, h, mod, 2, layer=l, n_batch=B, seq=S, tm=tm_rows, tn=512)
        n2 = norm_mod(h1, norm2_w, mod, 3, 4, layer=l, rows=n_rows, n_batch=B, seq=S)
        act = ffn_up_conv_gate(n2, w_ffn_up, ffn_conv_w, conv_b3, layer=l, n_x=NX, seq=S, n_ctx=NC, tm=tm_rows, tn=256)
        h = matmul_residual(act, w_down_bf, h1, mod, 5, layer=l, n_batch=B, seq=S, tm=tm_rows, tn=512,
                            tk=ffn_dim // 2)
    return h.reshape(B, S, D)
```
